```python
import jax, jax.numpy as jnp
from jax import lax
import numpy as np

D_MODEL = 4096
BATCH = 4
SEQ = 4096
DEPTH = 1

CHUNK = 64
N_MEM = 256
EPS = 1e-6

SSD_D_INNER = D_MODEL
SSD_HEAD_DIM = 64
SSD_N_HEADS = SSD_D_INNER // SSD_HEAD_DIM
SSD_N_GROUPS = 8
SSD_HEADS_PER_GROUP = SSD_N_HEADS // SSD_N_GROUPS
SSD_D_STATE = 128
SSD_CONV_WIDTH = 4
SSD_CONV_DIM = SSD_D_INNER + 2 * SSD_N_GROUPS * SSD_D_STATE

SB_HEAD_DIM = 128
SB_N_HEADS = 32
SB_D_INNER = SB_N_HEADS * SB_HEAD_DIM
SB_Q_BLOCK = 128

XA_N_HEADS = 4
XA_HEAD_DIM = D_MODEL // XA_N_HEADS

D_FF = ((8 * D_MODEL // 3 + 255) // 256) * 256

IN_SIZES = (SSD_D_INNER, SSD_CONV_DIM, SSD_N_HEADS, SB_D_INNER, SB_D_INNER, SB_D_INNER, 2 * D_MODEL)
D_IN_PROJ = sum(IN_SIZES)
IN_OFFSETS = tuple(int(v) for v in np.cumsum(IN_SIZES)[:-1])

kernel_name = "hybrid_ssd_stickbreaking_griffin_merge"


def rmsnorm(x, w):
    xf = x.astype(jnp.float32)
    y = xf * lax.rsqrt(jnp.mean(xf * xf, axis=-1, keepdims=True) + EPS)
    return (y * w.astype(jnp.float32)).astype(x.dtype)


def gated_group_rmsnorm(y, z, w):
    b, S, d = y.shape
    g = (y * jax.nn.silu(z)).astype(jnp.float32).reshape(b, S, SSD_N_GROUPS, d // SSD_N_GROUPS)
    g = g * lax.rsqrt(jnp.mean(g * g, axis=-1, keepdims=True) + EPS)
    return (g.reshape(b, S, d) * w.astype(jnp.float32)).astype(y.dtype)


def causal_depthwise_conv(u, w, bias):
    C = u.shape[-1]
    out = lax.conv_general_dilated(
        u, w[:, None, :].astype(u.dtype), window_strides=(1,), padding=[(SSD_CONV_WIDTH - 1, 0)],
        dimension_numbers=("NWC", "WIO", "NWC"), feature_group_count=C)
    return out + bias.astype(u.dtype)


def ssd_chunked_scan(xs, dt, A, Bm, Cm):
    b, S, H, P = xs.shape
    G, HPG, N = SSD_N_GROUPS, SSD_HEADS_PER_GROUP, SSD_D_STATE
    c = S // CHUNK
    dtype = xs.dtype
    x_c = xs.reshape(b, c, CHUNK, G, HPG, P)
    dt_c = dt.reshape(b, c, CHUNK, G, HPG)
    B_c = Bm.reshape(b, c, CHUNK, G, N)
    C_c = Cm.reshape(b, c, CHUNK, G, N)
    a_cum = jnp.cumsum(dt_c * A.reshape(G, HPG), axis=2)
    Xd = x_c * dt_c[..., None].astype(dtype)
    diff = a_cum[:, :, :, None] - a_cum[:, :, None, :]
    causal = jnp.tril(jnp.ones((CHUNK, CHUNK), dtype=bool))[:, :, None, None]
    L = jnp.exp(jnp.where(causal, diff, -jnp.inf)).astype(dtype)
    CB = jnp.einsum('bclgn,bcsgn->bclsg', C_c, B_c)
    y_diag = jnp.einsum('bclsg,bclsgh,bcsghp->bclghp', CB, L, Xd)
    decay_to_end = jnp.exp(a_cum[:, :, -1:] - a_cum).astype(dtype)
    states = jnp.einsum('bclgn,bclgh,bclghp->bcghpn', B_c, decay_to_end, Xd)
    chunk_decay = jnp.exp(a_cum[:, :, -1]).astype(dtype)

    def step(h, inp):
        st, dec = inp
        return h * dec[..., None, None] + st, h

    h0 = jnp.zeros((b, G, HPG, P, N), dtype)
    _, prev = lax.scan(step, h0, (jnp.moveaxis(states, 1, 0), jnp.moveaxis(chunk_decay, 1, 0)))
    prev = jnp.moveaxis(prev, 0, 1)
    y_off = jnp.einsum('bclgn,bcghpn,bclgh->bclghp', C_c, prev, jnp.exp(a_cum).astype(dtype))
    return (y_diag + y_off).reshape(b, S, H, P)


def stick_breaking_attention(q, k, v):
    S = q.shape[1]
    scale = SB_HEAD_DIM ** -0.5
    outs = []
    for i in range(S // SB_Q_BLOCK):
        q0 = i * SB_Q_BLOCK
        kend = q0 + SB_Q_BLOCK
        z = jnp.einsum('bqhd,bkhd->bhqk', q[:, q0:kend], k[:, :kend]).astype(jnp.float32) * scale
        t_idx = q0 + jnp.arange(SB_Q_BLOCK)[:, None]
        s_idx = jnp.arange(kend)[None, :]
        before = s_idx < t_idx
        log_keep = jnp.where(before, jax.nn.log_sigmoid(-z), 0.0)
        cum = jnp.cumsum(log_keep, axis=-1)
        log_w = jax.nn.log_sigmoid(z) + (cum[..., -1:] - cum)
        w = jnp.where(before, jnp.exp(log_w), 0.0).astype(v.dtype)
        outs.append(jnp.einsum('bhqk,bkhd->bqhd', w, v[:, :kend]))
    return jnp.concatenate(outs, axis=1)


def memory_cross_attention(h, m, w_q, w_kv, w_o):
    b, S, _ = h.shape
    q = (h @ w_q).reshape(b, S, XA_N_HEADS, XA_HEAD_DIM)
    k, v = jnp.split(m @ w_kv, 2, axis=-1)
    k = k.reshape(b, -1, XA_N_HEADS, XA_HEAD_DIM)
    v = v.reshape(b, -1, XA_N_HEADS, XA_HEAD_DIM)
    s = jnp.einsum('bqhd,bkhd->bhqk', q, k).astype(jnp.float32) * (XA_HEAD_DIM ** -0.5)
    p = jax.nn.softmax(s, axis=-1).astype(v.dtype)
    o = jnp.einsum('bhqk,bkhd->bqhd', p, v).reshape(b, S, D_MODEL)
    return o @ w_o


def swiglu_ffn(h, w_in, w_out):
    g, u = jnp.split(h @ w_in, 2, axis=-1)
    return (jax.nn.silu(g) * u) @ w_out


def setup_inputs(seed: int = 0) -> dict:
    key = jax.random.key(seed)
    ks = jax.random.split(key, 24)
    f32 = jnp.float32
    L = DEPTH

    def nrm(k, shape, scale):
        return jax.random.normal(k, shape, f32) * scale

    def gain(k, shape):
        return 1.0 + 0.02 * jax.random.normal(k, shape, f32)

    dt_init = jnp.exp(jax.random.uniform(ks[7], (L, SSD_N_HEADS), f32, np.log(1e-3), np.log(1e-1)))
    dt_bias = dt_init + jnp.log(-jnp.expm1(-dt_init))
    return {
        "x": nrm(ks[0], (BATCH, SEQ, D_MODEL), 1.0),
        "mem": nrm(ks[1], (BATCH, N_MEM, D_MODEL), 1.0),
        "norm_mix": gain(ks[2], (L, D_MODEL)),
        "w_in": nrm(ks[3], (L, D_MODEL, D_IN_PROJ), D_MODEL ** -0.5),
        "b_gate": nrm(ks[4], (L, 2 * D_MODEL), 0.02),
        "conv_w": nrm(ks[5], (L, SSD_CONV_WIDTH, SSD_CONV_DIM), SSD_CONV_WIDTH ** -0.5),
        "conv_b": nrm(ks[6], (L, SSD_CONV_DIM), 0.02),
        "dt_bias": dt_bias,
        "a_log": jnp.log(jax.random.uniform(ks[8], (L, SSD_N_HEADS), f32, 1.0, 16.0)),
        "d_skip": gain(ks[9], (L, SSD_N_HEADS)),
        "ssd_norm": gain(ks[10], (L, SSD_D_INNER)),
        "w_ssd_out": nrm(ks[11], (L, SSD_D_INNER, D_MODEL), SSD_D_INNER ** -0.5),
        "w_sb_out": nrm(ks[12], (L, SB_D_INNER, D_MODEL), SB_D_INNER ** -0.5),
        "w_out": nrm(ks[13], (L, D_MODEL, D_MODEL), D_MODEL ** -0.5),
        "norm_xa": gain(ks[14], (L, D_MODEL)),
        "norm_mem": gain(ks[15], (L, D_MODEL)),
        "w_xa_q": nrm(ks[16], (L, D_MODEL, D_MODEL), D_MODEL ** -0.5),
        "w_xa_kv": nrm(ks[17], (L, D_MODEL, 2 * D_MODEL), D_MODEL ** -0.5),
        "w_xa_o": nrm(ks[18], (L, D_MODEL, D_MODEL), D_MODEL ** -0.5),
        "norm_ffn": gain(ks[19], (L, D_MODEL)),
        "w_ffn_in": nrm(ks[20], (L, D_MODEL, 2 * D_FF), D_MODEL ** -0.5),
        "w_ffn_out": nrm(ks[21], (L, D_FF, D_MODEL), D_FF ** -0.5),
        "norm_final": gain(ks[22], (D_MODEL,)),
    }


def reference(x, mem, norm_mix, w_in, b_gate, conv_w, conv_b, dt_bias, a_log, d_skip, ssd_norm,
              w_ssd_out, w_sb_out, w_out, norm_xa, norm_mem, w_xa_q, w_xa_kv, w_xa_o,
              norm_ffn, w_ffn_in, w_ffn_out, norm_final):
    b, S, _ = x.shape
    for l in range(DEPTH):
        h = rmsnorm(x, norm_mix[l])
        proj = h @ w_in[l]
        z, xbc, dt_raw, q, k, v, gate_raw = jnp.split(proj, IN_OFFSETS, axis=-1)

        xbc = jax.nn.silu(causal_depthwise_conv(xbc, conv_w[l], conv_b[l]))
        xs, Bm, Cm = jnp.split(xbc, [SSD_D_INNER, SSD_D_INNER + SSD_N_GROUPS * SSD_D_STATE], axis=-1)
        xs = xs.reshape(b, S, SSD_N_HEADS, SSD_HEAD_DIM)
        Bm = Bm.reshape(b, S, SSD_N_GROUPS, SSD_D_STATE)
        Cm = Cm.reshape(b, S, SSD_N_GROUPS, SSD_D_STATE)
        dt = jax.nn.softplus(dt_raw.astype(jnp.float32) + dt_bias[l].astype(jnp.float32))
        A = -jnp.exp(a_log[l].astype(jnp.float32))
        y = ssd_chunked_scan(xs, dt, A, Bm, Cm) + d_skip[l][:, None] * xs
        y = gated_group_rmsnorm(y.reshape(b, S, SSD_D_INNER), z, ssd_norm[l])
        branch_ssd = y @ w_ssd_out[l]

        o = stick_breaking_attention(q.reshape(b, S, SB_N_HEADS, SB_HEAD_DIM),
                                     k.reshape(b, S, SB_N_HEADS, SB_HEAD_DIM),
                                     v.reshape(b, S, SB_N_HEADS, SB_HEAD_DIM))
        branch_sb = o.reshape(b, S, SB_D_INNER) @ w_sb_out[l]

        g_ssd, g_sb = jnp.split(jax.nn.sigmoid(gate_raw + b_gate[l]), 2, axis=-1)
        x = x + (g_ssd * branch_ssd + g_sb * branch_sb) @ w_out[l]

        x = x + memory_cross_attention(rmsnorm(x, norm_xa[l]), rmsnorm(mem, norm_mem[l]),
                                       w_xa_q[l], w_xa_kv[l], w_xa_o[l])

        x = x + swiglu_ffn(rmsnorm(x, norm_ffn[l]), w_ffn_in[l], w_ffn_out[l])
    return rmsnorm(x, norm_final)
```

```python
import functools

import jax
import jax.numpy as jnp
from jax import lax
from jax.experimental import pallas as pl
from jax.experimental.pallas import tpu as pltpu

F32 = jnp.float32
BF16 = jnp.bfloat16
EPS = 1e-6

V7X_VMEM_LIMIT_BYTES = 56 * 1024 * 1024

CHUNK = 64
SSD_HEAD_DIM = 64
SSD_N_HEADS = 64
SSD_N_GROUPS = 8
SSD_D_STATE = 128
SSD_D_INNER = SSD_N_HEADS * SSD_HEAD_DIM
SSD_GROUP_WIDTH = SSD_D_INNER // SSD_N_GROUPS
SSD_CONV_WIDTH = 4
SSD_CONV_DIM = SSD_D_INNER + 2 * SSD_N_GROUPS * SSD_D_STATE
DT_PAD = 128
SB_HEAD_DIM = 128
XA_N_HEADS = 4


def _params(*semantics):
    return pltpu.CompilerParams(dimension_semantics=semantics,
                                vmem_limit_bytes=V7X_VMEM_LIMIT_BYTES)


def _dot(a, b):
    return jnp.dot(a, b, preferred_element_type=F32)


def _dot_nt(a, b):
    return lax.dot_general(a, b, (((1,), (1,)), ((), ())), preferred_element_type=F32)


def _sigmoid(x):
    return 1.0 / (1.0 + jnp.exp(-x))


def _silu(x):
    return x * _sigmoid(x)


def _split3(v):
    p1 = v.astype(BF16)
    r1 = v - p1.astype(F32)
    p2 = r1.astype(BF16)
    p3 = (r1 - p2.astype(F32)).astype(BF16)
    return p1, p2, p3


def _rmsnorm_kernel(x_ref, w_ref, o_ref):
    x = x_ref[...]
    ms = jnp.mean(x * x, axis=-1, keepdims=True)
    o_ref[...] = (x * lax.rsqrt(ms + EPS) * w_ref[...]).astype(o_ref.dtype)


def _rmsnorm(x, w, out_dtype, tm=256):
    t, d = x.shape
    return pl.pallas_call(
        _rmsnorm_kernel,
        grid=(t // tm,),
        in_specs=[pl.BlockSpec((tm, d), lambda i: (i, 0)),
                  pl.BlockSpec((1, d), lambda i: (0, 0))],
        out_specs=pl.BlockSpec((tm, d), lambda i: (i, 0)),
        out_shape=jax.ShapeDtypeStruct((t, d), out_dtype),
        compiler_params=_params("parallel"),
        name="rmsnorm",
    )(x, w.reshape(1, d))


def _mm_kernel(a_ref, b_ref, o_ref):
    o_ref[...] = _dot(a_ref[...], b_ref[...]).astype(o_ref.dtype)


def _mm_res_kernel(a_ref, b_ref, r_ref, o_ref):
    o_ref[...] = (r_ref[...] + _dot(a_ref[...], b_ref[...])).astype(o_ref.dtype)


def _matmul(a, b, out_dtype, tm, tn, residual=None, name="matmul"):
    m, k = a.shape
    n = b.shape[1]
    in_specs = [pl.BlockSpec((tm, k), lambda i, j: (i, 0)),
                pl.BlockSpec((k, tn), lambda i, j: (0, j))]
    args = [a, b]
    kern = _mm_kernel
    if residual is not None:
        in_specs.append(pl.BlockSpec((tm, tn), lambda i, j: (i, j)))
        args.append(residual)
        kern = _mm_res_kernel
    return pl.pallas_call(
        kern,
        grid=(m // tm, n // tn),
        in_specs=in_specs,
        out_specs=pl.BlockSpec((tm, tn), lambda i, j: (i, j)),
        out_shape=jax.ShapeDtypeStruct((m, n), out_dtype),
        compiler_params=_params("parallel", "arbitrary"),
        name=name,
    )(*args)


def _mm_ksplit_res_kernel(a_ref, b_ref, r_ref, o_ref):
    @pl.when(pl.program_id(2) == 0)
    def _():
        o_ref[...] = r_ref[...]
    o_ref[...] += _dot(a_ref[...], b_ref[...])


def _matmul_ksplit_res(a, b, residual, tm, tn, tk, name="matmul_ksplit"):
    m, k = a.shape
    n = b.shape[1]
    return pl.pallas_call(
        _mm_ksplit_res_kernel,
        grid=(m // tm, n // tn, k // tk),
        in_specs=[pl.BlockSpec((tm, tk), lambda i, j, kk: (i, kk)),
                  pl.BlockSpec((tk, tn), lambda i, j, kk: (kk, j)),
                  pl.BlockSpec((tm, tn), lambda i, j, kk: (i, j))],
        out_specs=pl.BlockSpec((tm, tn), lambda i, j, kk: (i, j)),
        out_shape=jax.ShapeDtypeStruct((m, n), F32),
        compiler_params=_params("parallel", "arbitrary", "arbitrary"),
        name=name,
    )(a, b, residual)


def _merge_kernel(y_ref, o_ref, ws_ref, wb_ref, g1_ref, g2_ref, b1_ref, b2_ref, out_ref):
    branch_ssd = _dot(y_ref[...], ws_ref[...])
    branch_sb = _dot(o_ref[...], wb_ref[...])
    g_ssd = _sigmoid(g1_ref[...] + b1_ref[...])
    g_sb = _sigmoid(g2_ref[...] + b2_ref[...])
    out_ref[...] = (g_ssd * branch_ssd + g_sb * branch_sb).astype(out_ref.dtype)


def _merge(y, o, ws, wb, gate, b_gate, tm=512, tn=512):
    t, d = y.shape
    nj = d // tn
    return pl.pallas_call(
        _merge_kernel,
        grid=(t // tm, nj),
        in_specs=[pl.BlockSpec((tm, d), lambda i, j: (i, 0)),
                  pl.BlockSpec((tm, d), lambda i, j: (i, 0)),
                  pl.BlockSpec((d, tn), lambda i, j: (0, j)),
                  pl.BlockSpec((d, tn), lambda i, j: (0, j)),
                  pl.BlockSpec((tm, tn), lambda i, j: (i, j)),
                  pl.BlockSpec((tm, tn), lambda i, j: (i, j + nj)),
                  pl.BlockSpec((1, tn), lambda i, j: (0, j)),
                  pl.BlockSpec((1, tn), lambda i, j: (0, j + nj))],
        out_specs=pl.BlockSpec((tm, tn), lambda i, j: (i, j)),
        out_shape=jax.ShapeDtypeStruct((t, d), BF16),
        compiler_params=_params("parallel", "arbitrary"),
        name="merge",
    )(y, o, ws, wb, gate, gate, b_gate, b_gate)


def _swiglu_kernel(h_ref, wg_ref, wu_ref, o_ref):
    h = h_ref[...]
    g = _dot(h, wg_ref[...])
    u = _dot(h, wu_ref[...])
    o_ref[...] = (_silu(g) * u).astype(o_ref.dtype)


def _swiglu(h, wg, wu, tm=1024, tn=256):
    t, d = h.shape
    f = wg.shape[1]
    return pl.pallas_call(
        _swiglu_kernel,
        grid=(t // tm, f // tn),
        in_specs=[pl.BlockSpec((tm, d), lambda i, j: (i, 0)),
                  pl.BlockSpec((d, tn), lambda i, j: (0, j)),
                  pl.BlockSpec((d, tn), lambda i, j: (0, j))],
        out_specs=pl.BlockSpec((tm, tn), lambda i, j: (i, j)),
        out_shape=jax.ShapeDtypeStruct((t, f), BF16),
        compiler_params=_params("parallel", "arbitrary"),
        name="swiglu",
    )(h, wg, wu)


def _ssd_kernel(xbc_ref, z_ref, dt_ref, cw_ref, cb_ref, dtb_ref, alog_ref, dskip_ref, nw_ref, e_ref,
                y_ref, ext_ref, state_ref):
    L = CHUNK
    GW = SSD_GROUP_WIDTH
    N = SSD_D_STATE

    @pl.when(pl.program_id(1) == 0)
    def _():
        state_ref[...] = jnp.zeros_like(state_ref)
        ext_ref[0:8, :] = jnp.zeros((8, SSD_CONV_DIM), F32)

    ext_ref[8:8 + L, :] = xbc_ref[...]

    def conv_silu(c0, width):
        acc = cb_ref[:, c0:c0 + width]
        for k in range(SSD_CONV_WIDTH):
            r0 = 8 - (SSD_CONV_WIDTH - 1) + k
            acc = acc + cw_ref[k:k + 1, c0:c0 + width] * ext_ref[r0:r0 + L, c0:c0 + width]
        return _silu(acc)

    dt_in = dt_ref[...] + dtb_ref[...]
    dt = jnp.maximum(dt_in, 0.0) + jnp.log1p(jnp.exp(-jnp.abs(dt_in)))
    dt_pieces = jnp.concatenate(_split3(dt), axis=0)

    row = lax.broadcasted_iota(jnp.int32, (L, GW), 0)
    col = lax.broadcasted_iota(jnp.int32, (L, GW), 1)
    col_in_head = col & (SSD_HEAD_DIM - 1)
    causal = row >= col_in_head
    diag = row == col_in_head
    head4 = lax.broadcasted_iota(jnp.int32, (L, 4 * SSD_HEAD_DIM), 1) >> 6

    tl = lax.broadcasted_iota(jnp.int32, (L, L), 0)
    ts = lax.broadcasted_iota(jnp.int32, (L, L), 1)
    tril = (tl >= ts).astype(BF16)
    tril3 = jnp.concatenate([tril, tril, tril], axis=1)

    for g in range(SSD_N_GROUPS):
        c0 = g * GW
        xs = conv_silu(c0, GW)
        bm = conv_silu(SSD_D_INNER + g * N, N)
        cm = conv_silu(SSD_D_INNER + SSD_N_GROUPS * N + g * N, N)

        dt3 = _dot(dt_pieces, e_ref[:, c0:c0 + GW])
        dt_x = dt3[0:L] + dt3[L:2 * L] + dt3[2 * L:3 * L]
        a_x = dt_x * (-jnp.exp(alog_ref[:, c0:c0 + GW]))
        a_cum = _dot(tril3, jnp.concatenate(_split3(a_x), axis=0))
        a_last = a_cum[L - 1:L, :]
        exp_a = jnp.exp(a_cum)
        decay_to_end = jnp.exp(a_last - a_cum)

        xd = xs * dt_x

        a_at_s = jnp.sum(jnp.where(diag, a_cum, 0.0), axis=0, keepdims=True)
        lmat = jnp.where(causal, jnp.exp(jnp.minimum(a_cum - a_at_s, 0.0)), 0.0)
        bm_b = bm.astype(BF16)
        cm_b = cm.astype(BF16)
        b_tiled = jnp.concatenate([bm_b] * (GW // L), axis=0)
        cb = _dot_nt(cm_b, b_tiled)
        m_b = (cb * lmat).astype(BF16)

        y_parts = []
        for q in range(GW // (4 * SSD_HEAD_DIM)):
            q0 = q * 4 * SSD_HEAD_DIM
            x4 = xd[:, q0:q0 + 4 * SSD_HEAD_DIM]
            blockdiag = jnp.concatenate(
                [jnp.where(head4 == r, x4, 0.0).astype(BF16) for r in range(4)], axis=0)
            y_parts.append(_dot(m_b[:, q0:q0 + 4 * SSD_HEAD_DIM], blockdiag))
        y_diag = jnp.concatenate(y_parts, axis=1)

        prev = state_ref[:, c0:c0 + GW]
        y_off = _dot(cm_b, prev.astype(BF16)) * exp_a
        chunk_state = _dot(bm.T.astype(BF16), (xd * decay_to_end).astype(BF16))
        state_ref[:, c0:c0 + GW] = prev * jnp.exp(a_last) + chunk_state

        y = y_diag + y_off + dskip_ref[:, c0:c0 + GW] * xs

        gated = y * _silu(z_ref[:, c0:c0 + GW])
        ms = jnp.mean(gated * gated, axis=-1, keepdims=True)
        y_ref[:, c0:c0 + GW] = (gated * lax.rsqrt(ms + EPS) * nw_ref[:, c0:c0 + GW]).astype(y_ref.dtype)

    ext_ref[0:8, :] = xbc_ref[L - 8:L, :]


def _ssd(z, xbc, dt_raw, conv_w, conv_b, dt_bias, a_log, d_skip, ssd_norm, batch, seq):
    t = batch * seq
    nc = seq // CHUNK
    d = SSD_D_INNER
    head_of_channel = jnp.arange(d, dtype=jnp.int32) // SSD_HEAD_DIM
    expand = (jnp.arange(DT_PAD, dtype=jnp.int32)[:, None] == head_of_channel[None, :]).astype(BF16)
    per_channel = lambda v: jnp.repeat(v.astype(F32), SSD_HEAD_DIM).reshape(1, d)
    dtb = jnp.pad(dt_bias.astype(F32), (0, DT_PAD - SSD_N_HEADS)).reshape(1, DT_PAD)
    const = lambda shape: pl.BlockSpec(shape, lambda b, c: (0, 0))
    return pl.pallas_call(
        _ssd_kernel,
        grid=(batch, nc),
        in_specs=[pl.BlockSpec((CHUNK, SSD_CONV_DIM), lambda b, c: (b * nc + c, 0)),
                  pl.BlockSpec((CHUNK, d), lambda b, c: (b * nc + c, 0)),
                  pl.BlockSpec((CHUNK, DT_PAD), lambda b, c: (b * nc + c, 0)),
                  const((SSD_CONV_WIDTH, SSD_CONV_DIM)),
                  const((1, SSD_CONV_DIM)),
                  const((1, DT_PAD)),
                  const((1, d)),
                  const((1, d)),
                  const((1, d)),
                  const((DT_PAD, d))],
        out_specs=pl.BlockSpec((CHUNK, d), lambda b, c: (b * nc + c, 0)),
        out_shape=jax.ShapeDtypeStruct((t, d), BF16),
        scratch_shapes=[pltpu.VMEM((8 + CHUNK, SSD_CONV_DIM), F32),
                        pltpu.VMEM((SSD_D_STATE, d), F32)],
        compiler_params=_params("parallel", "arbitrary"),
        name="ssd",
    )(xbc, z, dt_raw, conv_w.astype(F32), conv_b.astype(F32).reshape(1, SSD_CONV_DIM), dtb,
      per_channel(a_log), per_channel(d_skip), ssd_norm.astype(F32).reshape(1, d), expand)


def _sb_kernel(q_ref, k_ref, v_ref, o_ref, *, tb):
    i = pl.program_id(2)
    scale = SB_HEAD_DIM ** -0.5
    q = q_ref[...]
    row = lax.broadcasted_iota(jnp.int32, (tb, tb), 0)
    col = lax.broadcasted_iota(jnp.int32, (tb, tb), 1)
    upper = (row > col).astype(BF16)
    upper2 = jnp.concatenate([upper, upper], axis=0)
    before = col < row

    def tile(j, acc, r, masked):
        start = pl.multiple_of(j * tb, tb)
        ks = k_ref[pl.ds(start, tb), :]
        vs = v_ref[pl.ds(start, tb), :]
        z = _dot_nt(q, ks) * scale
        log_beta = jnp.minimum(z, 0.0) - jnp.log1p(jnp.exp(-jnp.abs(z)))
        log_keep = log_beta - z
        if masked:
            log_keep = jnp.where(before, log_keep, 0.0)
        hi = log_keep.astype(BF16)
        lo = (log_keep - hi.astype(F32)).astype(BF16)
        suffix = _dot(jnp.concatenate([hi, lo], axis=1), upper2)
        w = jnp.exp(log_beta + suffix + r)
        if masked:
            w = jnp.where(before, w, 0.0)
        acc = acc + _dot(w.astype(BF16), vs)
        r = r + jnp.sum(log_keep, axis=-1, keepdims=True)
        return acc, r

    acc0 = jnp.zeros((tb, SB_HEAD_DIM), F32)
    r0 = jnp.zeros((tb, 1), F32)
    acc, r = tile(i, acc0, r0, True)

    def body(jj, carry):
        return tile(i - 1 - jj, carry[0], carry[1], False)

    acc, r = lax.fori_loop(0, i, body, (acc, r))
    o_ref[...] = acc.astype(o_ref.dtype)


def _sb_attention(qkv, batch, seq, n_heads, tb=256):
    t = batch * seq
    nq = seq // tb
    hd = SB_HEAD_DIM
    return pl.pallas_call(
        functools.partial(_sb_kernel, tb=tb),
        grid=(batch, n_heads, nq),
        in_specs=[pl.BlockSpec((tb, hd), lambda b, h, i: (b * nq + i, h)),
                  pl.BlockSpec((seq, hd), lambda b, h, i: (b, n_heads + h)),
                  pl.BlockSpec((seq, hd), lambda b, h, i: (b, 2 * n_heads + h))],
        out_specs=pl.BlockSpec((tb, hd), lambda b, h, i: (b * nq + i, h)),
        out_shape=jax.ShapeDtypeStruct((t, n_heads * hd), BF16),
        compiler_params=_params("parallel", "parallel", "arbitrary"),
        name="sb_attention",
    )(qkv, qkv, qkv)


def _xattn_kernel(q_ref, kv_ref, o_ref, *, n_heads, head_dim):
    scale = head_dim ** -0.5
    d = n_heads * head_dim
    for h in range(n_heads):
        c0 = h * head_dim
        s = _dot_nt(q_ref[:, c0:c0 + head_dim], kv_ref[:, c0:c0 + head_dim]) * scale
        p = jnp.exp(s - jnp.max(s, axis=-1, keepdims=True))
        p = p / jnp.sum(p, axis=-1, keepdims=True)
        o_ref[:, c0:c0 + head_dim] = _dot(p.astype(BF16), kv_ref[:, d + c0:d + c0 + head_dim]).astype(o_ref.dtype)


def _xattn(q, kv, batch, seq, n_mem, n_heads, tq=512):
    t, d = q.shape
    nq = seq // tq
    return pl.pallas_call(
        functools.partial(_xattn_kernel, n_heads=n_heads, head_dim=d // n_heads),
        grid=(batch, nq),
        in_specs=[pl.BlockSpec((tq, d), lambda b, i: (b * nq + i, 0)),
                  pl.BlockSpec((n_mem, 2 * d), lambda b, i: (b, 0))],
        out_specs=pl.BlockSpec((tq, d), lambda b, i: (b * nq + i, 0)),
        out_shape=jax.ShapeDtypeStruct((t, d), BF16),
        compiler_params=_params("parallel", "arbitrary"),
        name="xattn",
    )(q, kv)


def _layer(x, mem, norm_mix, w_in, b_gate, conv_w, conv_b, dt_bias, a_log, d_skip, ssd_norm,
           w_ssd_out, w_sb_out, w_out, norm_xa, norm_mem, w_xa_q, w_xa_kv, w_xa_o,
           norm_ffn, w_ffn_in, w_ffn_out, batch, seq):
    d = x.shape[1]
    n_mem = mem.shape[0] // batch
    d_sb = w_sb_out.shape[0]
    n_sb_heads = d_sb // SB_HEAD_DIM
    d_ff = w_ffn_out.shape[0]

    o_dt = SSD_D_INNER + SSD_CONV_DIM
    o_q = o_dt + SSD_N_HEADS
    o_gate = o_q + 3 * d_sb
    w_z = w_in[:, :SSD_D_INNER].astype(BF16)
    w_xbc = w_in[:, SSD_D_INNER:o_dt].astype(BF16)
    w_dt = jnp.pad(w_in[:, o_dt:o_q], ((0, 0), (0, DT_PAD - SSD_N_HEADS))).astype(BF16)
    w_qkv = w_in[:, o_q:o_gate].astype(BF16)
    w_gate = w_in[:, o_gate:].astype(BF16)

    h = _rmsnorm(x, norm_mix, BF16)
    z = _matmul(h, w_z, F32, 1024, 1024, name="proj_z")
    xbc = _matmul(h, w_xbc, F32, 1024, 1024, name="proj_xbc")
    dt_raw = _matmul(h, w_dt, F32, 1024, DT_PAD, name="proj_dt")
    qkv = _matmul(h, w_qkv, BF16, 1024, 1024, name="proj_qkv")
    gate = _matmul(h, w_gate, F32, 1024, 1024, name="proj_gate")

    y_ssd = _ssd(z, xbc, dt_raw, conv_w, conv_b, dt_bias, a_log, d_skip, ssd_norm, batch, seq)
    o_sb = _sb_attention(qkv, batch, seq, n_sb_heads)

    merged = _merge(y_ssd, o_sb, w_ssd_out.astype(BF16), w_sb_out.astype(BF16), gate,
                    b_gate.astype(F32).reshape(1, 2 * d))
    x = _matmul(merged, w_out.astype(BF16), F32, 1024, 512, residual=x, name="out_proj")

    hq = _rmsnorm(x, norm_xa, BF16)
    hm = _rmsnorm(mem, norm_mem, BF16)
    q = _matmul(hq, w_xa_q.astype(BF16), BF16, 1024, 1024, name="xa_q")
    kv = _matmul(hm, w_xa_kv.astype(BF16), BF16, mem.shape[0], 1024, name="xa_kv")
    o = _xattn(q, kv, batch, seq, n_mem, XA_N_HEADS)
    x = _matmul(o, w_xa_o.astype(BF16), F32, 1024, 512, residual=x, name="xa_o")

    hf = _rmsnorm(x, norm_ffn, BF16)
    act = _swiglu(hf, w_ffn_in[:, :d_ff].astype(BF16), w_ffn_in[:, d_ff:].astype(BF16))
    x = _matmul_ksplit_res(act, w_ffn_out.astype(BF16), x, 1024, 512, d_ff // 2, name="ffn_out")
    return x


def kernel(x, mem, norm_mix, w_in, b_gate, conv_w, conv_b, dt_bias, a_log, d_skip, ssd_norm, w_ssd_out, w_sb_out, w_out, norm_xa, norm_mem, w_xa_q, w_xa_kv, w_xa_o, norm_ffn, w_ffn_in, w_ffn_out, norm_final):
    batch, seq, d = x.shape
    xf = x.reshape(batch * seq, d)
    memf = mem.reshape(-1, d)
    for l in range(norm_mix.shape[0]):
        xf = _layer(xf, memf, norm_mix[l], w_in[l], b_gate[l], conv_w[l], conv_b[l], dt_bias[l], a_log[l],
                    d_skip[l], ssd_norm[l], w_ssd_out[l], w_sb_out[l], w_out[l], norm_xa[l], norm_mem[l],
                    w_xa_q[l], w_xa_kv[l], w_xa_o[l], norm_ffn[l], w_ffn_in[l], w_ffn_out[l], batch, seq)
    out = _rmsnorm(xf, norm_final, x.dtype)
    return out.reshape(batch, seq, d)
```

```python
import functools

import jax
import jax.numpy as jnp
from jax import lax
from jax.experimental import pallas as pl
from jax.experimental.pallas import tpu as pltpu

F32 = jnp.float32
BF16 = jnp.bfloat16
EPS = 1e-6

V7X_VMEM_LIMIT_BYTES = 56 * 1024 * 1024

CHUNK = 64
SSD_HEAD_DIM = 64
SSD_N_HEADS = 64
SSD_N_GROUPS = 8
SSD_D_STATE = 128
SSD_D_INNER = SSD_N_HEADS * SSD_HEAD_DIM
SSD_GROUP_WIDTH = SSD_D_INNER // SSD_N_GROUPS
SSD_CONV_WIDTH = 4
SSD_CONV_DIM = SSD_D_INNER + 2 * SSD_N_GROUPS * SSD_D_STATE
DT_PAD = 128
SB_HEAD_DIM = 128
SB_EXP_UNDERFLOW = -110.0
XA_N_HEADS = 4


def _params(*semantics):
    return pltpu.CompilerParams(dimension_semantics=semantics,
                                vmem_limit_bytes=V7X_VMEM_LIMIT_BYTES)


def _dot(a, b):
    return jnp.dot(a, b, preferred_element_type=F32)


def _dot_nt(a, b):
    return lax.dot_general(a, b, (((1,), (1,)), ((), ())), preferred_element_type=F32)


def _sigmoid(x):
    return 1.0 / (1.0 + jnp.exp(-x))


def _silu(x):
    return x * _sigmoid(x)


def _split3(v):
    p1 = v.astype(BF16)
    r1 = v - p1.astype(F32)
    p2 = r1.astype(BF16)
    p3 = (r1 - p2.astype(F32)).astype(BF16)
    return p1, p2, p3


def _rmsnorm_kernel(x_ref, w_ref, o_ref):
    x = x_ref[...]
    ms = jnp.mean(x * x, axis=-1, keepdims=True)
    o_ref[...] = (x * lax.rsqrt(ms + EPS) * w_ref[...]).astype(o_ref.dtype)


def _rmsnorm(x, w, out_dtype, tm=256):
    t, d = x.shape
    return pl.pallas_call(
        _rmsnorm_kernel,
        grid=(t // tm,),
        in_specs=[pl.BlockSpec((tm, d), lambda i: (i, 0)),
                  pl.BlockSpec((1, d), lambda i: (0, 0))],
        out_specs=pl.BlockSpec((tm, d), lambda i: (i, 0)),
        out_shape=jax.ShapeDtypeStruct((t, d), out_dtype),
        compiler_params=_params("parallel"),
        name="rmsnorm",
    )(x, w.reshape(1, d))


def _mm_kernel(a_ref, b_ref, o_ref):
    o_ref[...] = _dot(a_ref[...], b_ref[...]).astype(o_ref.dtype)


def _mm_res_kernel(a_ref, b_ref, r_ref, o_ref):
    o_ref[...] = (r_ref[...] + _dot(a_ref[...], b_ref[...])).astype(o_ref.dtype)


def _matmul(a, b, out_dtype, tm, tn, residual=None, name="matmul"):
    m, k = a.shape
    n = b.shape[1]
    in_specs = [pl.BlockSpec((tm, k), lambda i, j: (i, 0)),
                pl.BlockSpec((k, tn), lambda i, j: (0, j))]
    args = [a, b]
    kern = _mm_kernel
    if residual is not None:
        in_specs.append(pl.BlockSpec((tm, tn), lambda i, j: (i, j)))
        args.append(residual)
        kern = _mm_res_kernel
    return pl.pallas_call(
        kern,
        grid=(m // tm, n // tn),
        in_specs=in_specs,
        out_specs=pl.BlockSpec((tm, tn), lambda i, j: (i, j)),
        out_shape=jax.ShapeDtypeStruct((m, n), out_dtype),
        compiler_params=_params("parallel", "arbitrary"),
        name=name,
    )(*args)


def _mm_ksplit_res_kernel(a_ref, b_ref, r_ref, o_ref):
    @pl.when(pl.program_id(2) == 0)
    def _():
        o_ref[...] = r_ref[...]
    o_ref[...] += _dot(a_ref[...], b_ref[...])


def _matmul_ksplit_res(a, b, residual, tm, tn, tk, name="matmul_ksplit"):
    m, k = a.shape
    n = b.shape[1]
    return pl.pallas_call(
        _mm_ksplit_res_kernel,
        grid=(m // tm, n // tn, k // tk),
        in_specs=[pl.BlockSpec((tm, tk), lambda i, j, kk: (i, kk)),
                  pl.BlockSpec((tk, tn), lambda i, j, kk: (kk, j)),
                  pl.BlockSpec((tm, tn), lambda i, j, kk: (i, j))],
        out_specs=pl.BlockSpec((tm, tn), lambda i, j, kk: (i, j)),
        out_shape=jax.ShapeDtypeStruct((m, n), F32),
        compiler_params=_params("parallel", "arbitrary", "arbitrary"),
        name=name,
    )(a, b, residual)


def _merge_kernel(y_ref, o_ref, ws_ref, wb_ref, g1_ref, g2_ref, b1_ref, b2_ref, out_ref):
    branch_ssd = _dot(y_ref[...], ws_ref[...])
    branch_sb = _dot(o_ref[...], wb_ref[...])
    g_ssd = _sigmoid(g1_ref[...] + b1_ref[...])
    g_sb = _sigmoid(g2_ref[...] + b2_ref[...])
    out_ref[...] = (g_ssd * branch_ssd + g_sb * branch_sb).astype(out_ref.dtype)


def _merge(y, o, ws, wb, gate, b_gate, tm=512, tn=512):
    t, d = y.shape
    nj = d // tn
    return pl.pallas_call(
        _merge_kernel,
        grid=(t // tm, nj),
        in_specs=[pl.BlockSpec((tm, d), lambda i, j: (i, 0)),
                  pl.BlockSpec((tm, d), lambda i, j: (i, 0)),
                  pl.BlockSpec((d, tn), lambda i, j: (0, j)),
                  pl.BlockSpec((d, tn), lambda i, j: (0, j)),
                  pl.BlockSpec((tm, tn), lambda i, j: (i, j)),
                  pl.BlockSpec((tm, tn), lambda i, j: (i, j + nj)),
                  pl.BlockSpec((1, tn), lambda i, j: (0, j)),
                  pl.BlockSpec((1, tn), lambda i, j: (0, j + nj))],
        out_specs=pl.BlockSpec((tm, tn), lambda i, j: (i, j)),
        out_shape=jax.ShapeDtypeStruct((t, d), BF16),
        compiler_params=_params("parallel", "arbitrary"),
        name="merge",
    )(y, o, ws, wb, gate, gate, b_gate, b_gate)


def _swiglu_kernel(h_ref, wg_ref, wu_ref, o_ref):
    h = h_ref[...]
    g = _dot(h, wg_ref[...])
    u = _dot(h, wu_ref[...])
    o_ref[...] = (_silu(g) * u).astype(o_ref.dtype)


def _swiglu(h, wg, wu, tm=1024, tn=256):
    t, d = h.shape
    f = wg.shape[1]
    return pl.pallas_call(
        _swiglu_kernel,
        grid=(t // tm, f // tn),
        in_specs=[pl.BlockSpec((tm, d), lambda i, j: (i, 0)),
                  pl.BlockSpec((d, tn), lambda i, j: (0, j)),
                  pl.BlockSpec((d, tn), lambda i, j: (0, j))],
        out_specs=pl.BlockSpec((tm, tn), lambda i, j: (i, j)),
        out_shape=jax.ShapeDtypeStruct((t, f), BF16),
        compiler_params=_params("parallel", "arbitrary"),
        name="swiglu",
    )(h, wg, wu)


def _ssd_kernel(xbc_ref, z_ref, dt_ref, cw_ref, cb_ref, dtb_ref, alog_ref, dskip_ref, nw_ref, e_ref,
                y_ref, ext_ref, state_ref):
    L = CHUNK
    GW = SSD_GROUP_WIDTH
    N = SSD_D_STATE

    @pl.when(pl.program_id(1) == 0)
    def _():
        state_ref[...] = jnp.zeros_like(state_ref)
        ext_ref[0:8, :] = jnp.zeros((8, SSD_CONV_DIM), F32)

    ext_ref[8:8 + L, :] = xbc_ref[...]

    def conv_silu(c0, width):
        acc = cb_ref[:, c0:c0 + width]
        for k in range(SSD_CONV_WIDTH):
            r0 = 8 - (SSD_CONV_WIDTH - 1) + k
            acc = acc + cw_ref[k:k + 1, c0:c0 + width] * ext_ref[r0:r0 + L, c0:c0 + width]
        return _silu(acc)

    dt_in = dt_ref[...] + dtb_ref[...]
    dt = jnp.maximum(dt_in, 0.0) + jnp.log1p(jnp.exp(-jnp.abs(dt_in)))
    dt_pieces = jnp.concatenate(_split3(dt), axis=0)

    row = lax.broadcasted_iota(jnp.int32, (L, GW), 0)
    col = lax.broadcasted_iota(jnp.int32, (L, GW), 1)
    col_in_head = col & (SSD_HEAD_DIM - 1)
    causal = row >= col_in_head
    diag = row == col_in_head
    head4 = lax.broadcasted_iota(jnp.int32, (L, 4 * SSD_HEAD_DIM), 1) >> 6

    tl = lax.broadcasted_iota(jnp.int32, (L, L), 0)
    ts = lax.broadcasted_iota(jnp.int32, (L, L), 1)
    tril = (tl >= ts).astype(BF16)
    tril3 = jnp.concatenate([tril, tril, tril], axis=1)

    for g in range(SSD_N_GROUPS):
        c0 = g * GW
        xs = conv_silu(c0, GW)
        bm = conv_silu(SSD_D_INNER + g * N, N)
        cm = conv_silu(SSD_D_INNER + SSD_N_GROUPS * N + g * N, N)

        dt3 = _dot(dt_pieces, e_ref[:, c0:c0 + GW])
        dt_x = dt3[0:L] + dt3[L:2 * L] + dt3[2 * L:3 * L]
        a_x = dt_x * (-jnp.exp(alog_ref[:, c0:c0 + GW]))
        a_cum = _dot(tril3, jnp.concatenate(_split3(a_x), axis=0))
        a_last = a_cum[L - 1:L, :]
        exp_a = jnp.exp(a_cum)
        decay_to_end = jnp.exp(a_last - a_cum)

        xd = xs * dt_x

        a_at_s = jnp.sum(jnp.where(diag, a_cum, 0.0), axis=0, keepdims=True)
        lmat = jnp.where(causal, jnp.exp(jnp.minimum(a_cum - a_at_s, 0.0)), 0.0)
        bm_b = bm.astype(BF16)
        cm_b = cm.astype(BF16)
        b_tiled = jnp.concatenate([bm_b] * (GW // L), axis=0)
        cb = _dot_nt(cm_b, b_tiled)
        m_b = (cb * lmat).astype(BF16)

        y_parts = []
        for q in range(GW // (4 * SSD_HEAD_DIM)):
            q0 = q * 4 * SSD_HEAD_DIM
            x4 = xd[:, q0:q0 + 4 * SSD_HEAD_DIM]
            blockdiag = jnp.concatenate(
                [jnp.where(head4 == r, x4, 0.0).astype(BF16) for r in range(4)], axis=0)
            y_parts.append(_dot(m_b[:, q0:q0 + 4 * SSD_HEAD_DIM], blockdiag))
        y_diag = jnp.concatenate(y_parts, axis=1)

        prev = state_ref[:, c0:c0 + GW]
        y_off = _dot(cm_b, prev.astype(BF16)) * exp_a
        chunk_state = _dot(bm.T.astype(BF16), (xd * decay_to_end).astype(BF16))
        state_ref[:, c0:c0 + GW] = prev * jnp.exp(a_last) + chunk_state

        y = y_diag + y_off + dskip_ref[:, c0:c0 + GW] * xs

        gated = y * _silu(z_ref[:, c0:c0 + GW])
        ms = jnp.mean(gated * gated, axis=-1, keepdims=True)
        y_ref[:, c0:c0 + GW] = (gated * lax.rsqrt(ms + EPS) * nw_ref[:, c0:c0 + GW]).astype(y_ref.dtype)

    ext_ref[0:8, :] = xbc_ref[L - 8:L, :]


def _ssd(z, xbc, dt_raw, conv_w, conv_b, dt_bias, a_log, d_skip, ssd_norm, batch, seq):
    t = batch * seq
    nc = seq // CHUNK
    d = SSD_D_INNER
    head_of_channel = jnp.arange(d, dtype=jnp.int32) // SSD_HEAD_DIM
    expand = (jnp.arange(DT_PAD, dtype=jnp.int32)[:, None] == head_of_channel[None, :]).astype(BF16)
    per_channel = lambda v: jnp.repeat(v.astype(F32), SSD_HEAD_DIM).reshape(1, d)
    dtb = jnp.pad(dt_bias.astype(F32), (0, DT_PAD - SSD_N_HEADS)).reshape(1, DT_PAD)
    const = lambda shape: pl.BlockSpec(shape, lambda b, c: (0, 0))
    return pl.pallas_call(
        _ssd_kernel,
        grid=(batch, nc),
        in_specs=[pl.BlockSpec((CHUNK, SSD_CONV_DIM), lambda b, c: (b * nc + c, 0)),
                  pl.BlockSpec((CHUNK, d), lambda b, c: (b * nc + c, 0)),
                  pl.BlockSpec((CHUNK, DT_PAD), lambda b, c: (b * nc + c, 0)),
                  const((SSD_CONV_WIDTH, SSD_CONV_DIM)),
                  const((1, SSD_CONV_DIM)),
                  const((1, DT_PAD)),
                  const((1, d)),
                  const((1, d)),
                  const((1, d)),
                  const((DT_PAD, d))],
        out_specs=pl.BlockSpec((CHUNK, d), lambda b, c: (b * nc + c, 0)),
        out_shape=jax.ShapeDtypeStruct((t, d), BF16),
        scratch_shapes=[pltpu.VMEM((8 + CHUNK, SSD_CONV_DIM), F32),
                        pltpu.VMEM((SSD_D_STATE, d), F32)],
        compiler_params=_params("parallel", "arbitrary"),
        name="ssd",
    )(xbc, z, dt_raw, conv_w.astype(F32), conv_b.astype(F32).reshape(1, SSD_CONV_DIM), dtb,
      per_channel(a_log), per_channel(d_skip), ssd_norm.astype(F32).reshape(1, d), expand)


def _sb_kernel(q_ref, k_ref, v_ref, o_ref, *, tb, n_heads_per_step, n_fused):
    hd = SB_HEAD_DIM
    scale = hd ** -0.5
    nq = q_ref.shape[0] // tb
    heads = range(n_heads_per_step)
    row = lax.broadcasted_iota(jnp.int32, (tb, tb), 0)
    col = lax.broadcasted_iota(jnp.int32, (tb, tb), 1)
    upper = (row > col).astype(BF16)
    upper2 = jnp.concatenate([upper, upper], axis=0)
    before = col < row

    cols = [slice(h * hd, (h + 1) * hd) for h in heads]

    def key_blocks(i, blocks, accs, rs):
        q0 = pl.multiple_of(i * tb, tb)
        k0s = [pl.multiple_of(j * tb, tb) for j, _ in blocks]
        units = [(t, h) for t in range(len(blocks)) for h in heads]
        zs = [_dot_nt(q_ref[pl.ds(q0, tb), cols[h]], k_ref[pl.ds(k0s[t], tb), cols[h]]) * scale for t, h in units]
        log_betas = [jnp.minimum(z, 0.0) - jnp.log(1.0 + jnp.exp(-jnp.abs(z))) for z in zs]
        log_keeps = [lb - z for lb, z in zip(log_betas, zs)]
        log_keeps = [jnp.where(before, lk, 0.0) if blocks[t][1] else lk for lk, (t, _) in zip(log_keeps, units)]
        his = [lk.astype(BF16) for lk in log_keeps]
        los = [(lk - hi.astype(F32)).astype(BF16) for lk, hi in zip(log_keeps, his)]
        suffixes = [_dot(jnp.concatenate([hi, lo], axis=1), upper2) for hi, lo in zip(his, los)]
        partial = [lb + sf for lb, sf in zip(log_betas, suffixes)]
        row_sums = [jnp.sum(lk, axis=-1, keepdims=True) for lk in log_keeps]
        accs, rs = list(accs), list(rs)
        for u, (t, h) in enumerate(units):
            w = jnp.exp(partial[u] + rs[h])
            if blocks[t][1]:
                w = jnp.where(before, w, 0.0)
            accs[h] = accs[h] + _dot(w.astype(BF16), v_ref[pl.ds(k0s[t], tb), cols[h]])
            rs[h] = rs[h] + row_sums[u]
        return tuple(accs), tuple(rs)

    def still_live(rs):
        worst = functools.reduce(jnp.maximum, rs)
        return (jnp.max(worst) > SB_EXP_UNDERFLOW).astype(jnp.int32)

    def q_block(i, n_first):
        i = jnp.asarray(i, jnp.int32)
        accs = tuple(jnp.zeros((tb, hd), F32) for _ in heads)
        rs = tuple(jnp.zeros((tb, 1), F32) for _ in heads)
        accs, rs = key_blocks(i, [(i - t, t == 0) for t in range(n_first)], accs, rs)

        def cond(c):
            return jnp.logical_and(c[0] >= 0, c[1] > 0)

        def body(c):
            accs, rs = key_blocks(i, [(c[0], False)], c[2], c[3])
            return c[0] - 1, still_live(rs), accs, rs

        _, _, accs, rs = lax.while_loop(cond, body, (i - n_first, still_live(rs), accs, rs))
        q0 = pl.multiple_of(i * tb, tb)
        for h in heads:
            o_ref[pl.ds(q0, tb), cols[h]] = accs[h].astype(o_ref.dtype)

    for i in range(min(n_fused - 1, nq)):
        q_block(i, i + 1)

    def steady(i, carry):
        q_block(i, n_fused)
        return carry

    lax.fori_loop(n_fused - 1, nq, steady, 0)


def _sb_attention(qkv, batch, seq, n_heads, tb=128, n_heads_per_step=4, n_fused=3):
    t = batch * seq
    hw = n_heads_per_step * SB_HEAD_DIM
    ng = n_heads // n_heads_per_step
    return pl.pallas_call(
        functools.partial(_sb_kernel, tb=tb, n_heads_per_step=n_heads_per_step, n_fused=n_fused),
        grid=(batch, ng),
        in_specs=[pl.BlockSpec((seq, hw), lambda b, h: (b, h)),
                  pl.BlockSpec((seq, hw), lambda b, h: (b, ng + h)),
                  pl.BlockSpec((seq, hw), lambda b, h: (b, 2 * ng + h))],
        out_specs=pl.BlockSpec((seq, hw), lambda b, h: (b, h)),
        out_shape=jax.ShapeDtypeStruct((t, n_heads * SB_HEAD_DIM), BF16),
        compiler_params=_params("parallel", "parallel"),
        name="sb_attention",
    )(qkv, qkv, qkv)


def _xattn_kernel(q_ref, kv_ref, o_ref, *, n_heads, head_dim):
    scale = head_dim ** -0.5
    d = n_heads * head_dim
    for h in range(n_heads):
        c0 = h * head_dim
        s = _dot_nt(q_ref[:, c0:c0 + head_dim], kv_ref[:, c0:c0 + head_dim]) * scale
        p = jnp.exp(s - jnp.max(s, axis=-1, keepdims=True))
        p = p / jnp.sum(p, axis=-1, keepdims=True)
        o_ref[:, c0:c0 + head_dim] = _dot(p.astype(BF16), kv_ref[:, d + c0:d + c0 + head_dim]).astype(o_ref.dtype)


def _xattn(q, kv, batch, seq, n_mem, n_heads, tq=512):
    t, d = q.shape
    nq = seq // tq
    return pl.pallas_call(
        functools.partial(_xattn_kernel, n_heads=n_heads, head_dim=d // n_heads),
        grid=(batch, nq),
        in_specs=[pl.BlockSpec((tq, d), lambda b, i: (b * nq + i, 0)),
                  pl.BlockSpec((n_mem, 2 * d), lambda b, i: (b, 0))],
        out_specs=pl.BlockSpec((tq, d), lambda b, i: (b * nq + i, 0)),
        out_shape=jax.ShapeDtypeStruct((t, d), BF16),
        compiler_params=_params("parallel", "arbitrary"),
        name="xattn",
    )(q, kv)


def _layer(x, mem, norm_mix, w_in, b_gate, conv_w, conv_b, dt_bias, a_log, d_skip, ssd_norm,
           w_ssd_out, w_sb_out, w_out, norm_xa, norm_mem, w_xa_q, w_xa_kv, w_xa_o,
           norm_ffn, w_ffn_in, w_ffn_out, batch, seq):
    d = x.shape[1]
    n_mem = mem.shape[0] // batch
    d_sb = w_sb_out.shape[0]
    n_sb_heads = d_sb // SB_HEAD_DIM
    d_ff = w_ffn_out.shape[0]

    o_dt = SSD_D_INNER + SSD_CONV_DIM
    o_q = o_dt + SSD_N_HEADS
    o_gate = o_q + 3 * d_sb
    w_z = w_in[:, :SSD_D_INNER].astype(BF16)
    w_xbc = w_in[:, SSD_D_INNER:o_dt].astype(BF16)
    w_dt = jnp.pad(w_in[:, o_dt:o_q], ((0, 0), (0, DT_PAD - SSD_N_HEADS))).astype(BF16)
    w_qkv = w_in[:, o_q:o_gate].astype(BF16)
    w_gate = w_in[:, o_gate:].astype(BF16)

    h = _rmsnorm(x, norm_mix, BF16)
    z = _matmul(h, w_z, F32, 1024, 1024, name="proj_z")
    xbc = _matmul(h, w_xbc, F32, 1024, 1024, name="proj_xbc")
    dt_raw = _matmul(h, w_dt, F32, 1024, DT_PAD, name="proj_dt")
    qkv = _matmul(h, w_qkv, BF16, 1024, 1024, name="proj_qkv")
    gate = _matmul(h, w_gate, F32, 1024, 1024, name="proj_gate")

    y_ssd = _ssd(z, xbc, dt_raw, conv_w, conv_b, dt_bias, a_log, d_skip, ssd_norm, batch, seq)
    o_sb = _sb_attention(qkv, batch, seq, n_sb_heads)

    merged = _merge(y_ssd, o_sb, w_ssd_out.astype(BF16), w_sb_out.astype(BF16), gate,
                    b_gate.astype(F32).reshape(1, 2 * d))
    x = _matmul(merged, w_out.astype(BF16), F32, 1024, 512, residual=x, name="out_proj")

    hq = _rmsnorm(x, norm_xa, BF16)
    hm = _rmsnorm(mem, norm_mem, BF16)
    q = _matmul(hq, w_xa_q.astype(BF16), BF16, 1024, 1024, name="xa_q")
    kv = _matmul(hm, w_xa_kv.astype(BF16), BF16, mem.shape[0], 1024, name="xa_kv")
    o = _xattn(q, kv, batch, seq, n_mem, XA_N_HEADS)
    x = _matmul(o, w_xa_o.astype(BF16), F32, 1024, 512, residual=x, name="xa_o")

    hf = _rmsnorm(x, norm_ffn, BF16)
    act = _swiglu(hf, w_ffn_in[:, :d_ff].astype(BF16), w_ffn_in[:, d_ff:].astype(BF16))
    x = _matmul_ksplit_res(act, w_ffn_out.astype(BF16), x, 1024, 512, d_ff // 2, name="ffn_out")
    return x


def kernel(x, mem, norm_mix, w_in, b_gate, conv_w, conv_b, dt_bias, a_log, d_skip, ssd_norm, w_ssd_out, w_sb_out, w_out, norm_xa, norm_mem, w_xa_q, w_xa_kv, w_xa_o, norm_ffn, w_ffn_in, w_ffn_out, norm_final):
    batch, seq, d = x.shape
    xf = x.reshape(batch * seq, d)
    memf = mem.reshape(-1, d)
    for l in range(norm_mix.shape[0]):
        xf = _layer(xf, memf, norm_mix[l], w_in[l], b_gate[l], conv_w[l], conv_b[l], dt_bias[l], a_log[l],
                    d_skip[l], ssd_norm[l], w_ssd_out[l], w_sb_out[l], w_out[l], norm_xa[l], norm_mem[l],
                    w_xa_q[l], w_xa_kv[l], w_xa_o[l], norm_ffn[l], w_ffn_in[l], w_ffn_out[l], batch, seq)
    out = _rmsnorm(xf, norm_final, x.dtype)
    return out.reshape(batch, seq, d)
```

```python
import functools
import math

import jax
import jax.numpy as jnp
from jax import lax
from jax.experimental import pallas as pl
from jax.experimental.pallas import tpu as pltpu

F32 = jnp.float32
BF16 = jnp.bfloat16
EPS = 1e-6

V7X_VMEM_LIMIT_BYTES = 56 * 1024 * 1024

CHUNK = 64
SSD_HEAD_DIM = 64
SSD_N_HEADS = 64
SSD_N_GROUPS = 8
SSD_D_STATE = 128
SSD_D_INNER = SSD_N_HEADS * SSD_HEAD_DIM
SSD_GROUP_WIDTH = SSD_D_INNER // SSD_N_GROUPS
SSD_CONV_WIDTH = 4
SSD_CONV_DIM = SSD_D_INNER + 2 * SSD_N_GROUPS * SSD_D_STATE
DT_PAD = 128
SB_HEAD_DIM = 128
SB_EXP2_UNDERFLOW = -160.0
XA_N_HEADS = 4


def _params(*semantics):
    return pltpu.CompilerParams(dimension_semantics=semantics,
                                vmem_limit_bytes=V7X_VMEM_LIMIT_BYTES)


def _dot(a, b):
    return jnp.dot(a, b, preferred_element_type=F32)


def _dot_nt(a, b):
    return lax.dot_general(a, b, (((1,), (1,)), ((), ())), preferred_element_type=F32)


def _sigmoid(x):
    return 0.5 + 0.5 * jnp.tanh(0.5 * x)


def _silu(x):
    half = 0.5 * x
    return half + half * jnp.tanh(half)


def _split3(v):
    p1 = v.astype(BF16)
    r1 = v - p1.astype(F32)
    p2 = r1.astype(BF16)
    p3 = (r1 - p2.astype(F32)).astype(BF16)
    return p1, p2, p3


def _rmsnorm_kernel(x_ref, w_ref, o_ref):
    x = x_ref[...]
    ms = jnp.mean(x * x, axis=-1, keepdims=True)
    o_ref[...] = (x * lax.rsqrt(ms + EPS) * w_ref[...]).astype(o_ref.dtype)


def _rmsnorm(x, w, out_dtype, tm=256):
    t, d = x.shape
    return pl.pallas_call(
        _rmsnorm_kernel,
        grid=(t // tm,),
        in_specs=[pl.BlockSpec((tm, d), lambda i: (i, 0)),
                  pl.BlockSpec((1, d), lambda i: (0, 0))],
        out_specs=pl.BlockSpec((tm, d), lambda i: (i, 0)),
        out_shape=jax.ShapeDtypeStruct((t, d), out_dtype),
        compiler_params=_params("parallel"),
        name="rmsnorm",
    )(x, w.reshape(1, d))


def _mm_kernel(a_ref, b_ref, o_ref):
    o_ref[...] = _dot(a_ref[...], b_ref[...]).astype(o_ref.dtype)


def _mm_res_kernel(a_ref, b_ref, r_ref, o_ref):
    o_ref[...] = (r_ref[...] + _dot(a_ref[...], b_ref[...])).astype(o_ref.dtype)


def _matmul(a, b, out_dtype, tm, tn, residual=None, name="matmul", b_cols=None):
    m, k = a.shape
    col0, n = b_cols if b_cols is not None else (0, b.shape[1])
    assert col0 % tn == 0 and n % tn == 0 and m % tm == 0
    jb = col0 // tn
    in_specs = [pl.BlockSpec((tm, k), lambda i, j: (i, 0)),
                pl.BlockSpec((k, tn), lambda i, j: (0, j + jb))]
    args = [a, b]
    kern = _mm_kernel
    if residual is not None:
        in_specs.append(pl.BlockSpec((tm, tn), lambda i, j: (i, j)))
        args.append(residual)
        kern = _mm_res_kernel
    return pl.pallas_call(
        kern,
        grid=(m // tm, n // tn),
        in_specs=in_specs,
        out_specs=pl.BlockSpec((tm, tn), lambda i, j: (i, j)),
        out_shape=jax.ShapeDtypeStruct((m, n), out_dtype),
        compiler_params=_params("parallel", "arbitrary"),
        name=name,
    )(*args)


def _mm_ksplit_res_kernel(a_ref, b_ref, r_ref, o_ref):
    @pl.when(pl.program_id(2) == 0)
    def _():
        o_ref[...] = r_ref[...]
    o_ref[...] += _dot(a_ref[...], b_ref[...])


def _matmul_ksplit_res(a, b, residual, tm, tn, tk, name="matmul_ksplit"):
    m, k = a.shape
    n = b.shape[1]
    return pl.pallas_call(
        _mm_ksplit_res_kernel,
        grid=(m // tm, n // tn, k // tk),
        in_specs=[pl.BlockSpec((tm, tk), lambda i, j, kk: (i, kk)),
                  pl.BlockSpec((tk, tn), lambda i, j, kk: (kk, j)),
                  pl.BlockSpec((tm, tn), lambda i, j, kk: (i, j))],
        out_specs=pl.BlockSpec((tm, tn), lambda i, j, kk: (i, j)),
        out_shape=jax.ShapeDtypeStruct((m, n), F32),
        compiler_params=_params("parallel", "arbitrary", "arbitrary"),
        name=name,
    )(a, b, residual)


def _merge_kernel(y_ref, o_ref, ws_ref, wb_ref, g1_ref, g2_ref, b1_ref, b2_ref, out_ref):
    branch_ssd = _dot(y_ref[...], ws_ref[...])
    branch_sb = _dot(o_ref[...], wb_ref[...])
    g_ssd = _sigmoid(g1_ref[...] + b1_ref[...])
    g_sb = _sigmoid(g2_ref[...] + b2_ref[...])
    out_ref[...] = (g_ssd * branch_ssd + g_sb * branch_sb).astype(out_ref.dtype)


def _merge(y, o, ws, wb, gate, b_gate, tm=512, tn=512):
    t, d = y.shape
    nj = d // tn
    return pl.pallas_call(
        _merge_kernel,
        grid=(t // tm, nj),
        in_specs=[pl.BlockSpec((tm, d), lambda i, j: (i, 0)),
                  pl.BlockSpec((tm, d), lambda i, j: (i, 0)),
                  pl.BlockSpec((d, tn), lambda i, j: (0, j)),
                  pl.BlockSpec((d, tn), lambda i, j: (0, j)),
                  pl.BlockSpec((tm, tn), lambda i, j: (i, j)),
                  pl.BlockSpec((tm, tn), lambda i, j: (i, j + nj)),
                  pl.BlockSpec((1, tn), lambda i, j: (0, j)),
                  pl.BlockSpec((1, tn), lambda i, j: (0, j + nj))],
        out_specs=pl.BlockSpec((tm, tn), lambda i, j: (i, j)),
        out_shape=jax.ShapeDtypeStruct((t, d), BF16),
        compiler_params=_params("parallel", "arbitrary"),
        name="merge",
    )(y, o, ws, wb, gate, gate, b_gate, b_gate)


def _swiglu_kernel(h_ref, wg_ref, wu_ref, o_ref):
    h = h_ref[...]
    g = _dot(h, wg_ref[...])
    u = _dot(h, wu_ref[...])
    o_ref[...] = (_silu(g) * u).astype(o_ref.dtype)


def _swiglu(h, w, tm=1024, tn=256):
    t, d = h.shape
    f = w.shape[1] // 2
    assert f % tn == 0 and t % tm == 0
    nj = f // tn
    return pl.pallas_call(
        _swiglu_kernel,
        grid=(t // tm, nj),
        in_specs=[pl.BlockSpec((tm, d), lambda i, j: (i, 0)),
                  pl.BlockSpec((d, tn), lambda i, j: (0, j)),
                  pl.BlockSpec((d, tn), lambda i, j: (0, j + nj))],
        out_specs=pl.BlockSpec((tm, tn), lambda i, j: (i, j)),
        out_shape=jax.ShapeDtypeStruct((t, f), BF16),
        compiler_params=_params("parallel", "arbitrary"),
        name="swiglu",
    )(h, w, w)


def _ssd_kernel(xbc_ref, z_ref, dt_ref, cw_ref, cb_ref, dtb_ref, alog_ref, dskip_ref, nw_ref, e_ref,
                y_ref, ext_ref, state_ref):
    L = CHUNK
    GW = SSD_GROUP_WIDTH
    N = SSD_D_STATE

    @pl.when(pl.program_id(1) == 0)
    def _():
        state_ref[...] = jnp.zeros_like(state_ref)
        ext_ref[0:8, :] = jnp.zeros((8, SSD_CONV_DIM), F32)

    ext_ref[8:8 + L, :] = xbc_ref[...]

    def conv_silu(c0, width):
        acc = cb_ref[:, c0:c0 + width]
        for k in range(SSD_CONV_WIDTH):
            r0 = 8 - (SSD_CONV_WIDTH - 1) + k
            acc = acc + cw_ref[k:k + 1, c0:c0 + width] * ext_ref[r0:r0 + L, c0:c0 + width]
        return _silu(acc)

    tl = lax.broadcasted_iota(jnp.int32, (L, L), 0)
    ts = lax.broadcasted_iota(jnp.int32, (L, L), 1)
    tril = (tl >= ts).astype(BF16)
    tril3 = jnp.concatenate([tril, tril, tril], axis=1)

    dt_in = dt_ref[...] + dtb_ref[...]
    dt = jnp.maximum(dt_in, 0.0) + jnp.log1p(jnp.exp(-jnp.abs(dt_in)))
    a_step = dt * (-jnp.exp(alog_ref[...]))
    a_cum_h = _dot(tril3, jnp.concatenate(_split3(a_step), axis=0))
    head_pieces = jnp.concatenate(_split3(dt) + _split3(a_cum_h), axis=0)

    row = lax.broadcasted_iota(jnp.int32, (L, GW), 0)
    col = lax.broadcasted_iota(jnp.int32, (L, GW), 1)
    col_in_head = col & (SSD_HEAD_DIM - 1)
    causal = row >= col_in_head
    diag = row == col_in_head
    head4 = lax.broadcasted_iota(jnp.int32, (L, 4 * SSD_HEAD_DIM), 1) >> 6
    head_masks = [(head4 == r).astype(BF16) for r in range(4)]

    for g in range(SSD_N_GROUPS):
        c0 = g * GW
        xs = conv_silu(c0, GW)
        bm = conv_silu(SSD_D_INNER + g * N, N)
        cm = conv_silu(SSD_D_INNER + SSD_N_GROUPS * N + g * N, N)

        ex = _dot(head_pieces, e_ref[:, c0:c0 + GW])
        dt_x = ex[0:L] + ex[L:2 * L] + ex[2 * L:3 * L]
        a_cum = ex[3 * L:4 * L] + ex[4 * L:5 * L] + ex[5 * L:6 * L]
        a_last = a_cum[L - 1:L, :]
        exp_a = jnp.exp(a_cum)
        decay_to_end = jnp.exp(a_last - a_cum)

        xd = xs * dt_x

        a_at_s = jnp.sum(jnp.where(diag, a_cum, 0.0), axis=0, keepdims=True)
        lmat = jnp.where(causal, jnp.exp(jnp.minimum(a_cum - a_at_s, 0.0)), 0.0)
        bm_b = bm.astype(BF16)
        cm_b = cm.astype(BF16)
        b_tiled = jnp.concatenate([bm_b] * (GW // L), axis=0)
        cb = _dot_nt(cm_b, b_tiled)
        m_b = (cb * lmat).astype(BF16)

        xd_b = xd.astype(BF16)
        y_parts = []
        for q in range(GW // (4 * SSD_HEAD_DIM)):
            q0 = q * 4 * SSD_HEAD_DIM
            x4 = xd_b[:, q0:q0 + 4 * SSD_HEAD_DIM]
            blockdiag = jnp.concatenate([x4 * mask for mask in head_masks], axis=0)
            y_parts.append(_dot(m_b[:, q0:q0 + 4 * SSD_HEAD_DIM], blockdiag))
        y_diag = jnp.concatenate(y_parts, axis=1)

        prev = state_ref[:, c0:c0 + GW]
        y_off = _dot(cm_b, prev.astype(BF16)) * exp_a
        chunk_state = _dot(bm.T.astype(BF16), (xd * decay_to_end).astype(BF16))
        state_ref[:, c0:c0 + GW] = prev * jnp.exp(a_last) + chunk_state

        y = y_diag + y_off + dskip_ref[:, c0:c0 + GW] * xs

        gated = y * _silu(z_ref[:, c0:c0 + GW])
        ms = jnp.mean(gated * gated, axis=-1, keepdims=True)
        y_ref[:, c0:c0 + GW] = (gated * lax.rsqrt(ms + EPS) * nw_ref[:, c0:c0 + GW]).astype(y_ref.dtype)

    ext_ref[0:8, :] = xbc_ref[L - 8:L, :]


def _ssd(z, xbc, dt_raw, conv_w, conv_b, dt_bias, a_log, d_skip, ssd_norm, batch, seq):
    t = batch * seq
    nc = seq // CHUNK
    d = SSD_D_INNER
    head_of_channel = jnp.arange(d, dtype=jnp.int32) // SSD_HEAD_DIM
    expand = (jnp.arange(DT_PAD, dtype=jnp.int32)[:, None] == head_of_channel[None, :]).astype(BF16)
    per_channel = lambda v: jnp.repeat(v.astype(F32), SSD_HEAD_DIM).reshape(1, d)
    per_head = lambda v: jnp.pad(v.astype(F32), (0, DT_PAD - SSD_N_HEADS)).reshape(1, DT_PAD)
    const = lambda shape: pl.BlockSpec(shape, lambda b, c: (0, 0))
    return pl.pallas_call(
        _ssd_kernel,
        grid=(batch, nc),
        in_specs=[pl.BlockSpec((CHUNK, SSD_CONV_DIM), lambda b, c: (b * nc + c, 0)),
                  pl.BlockSpec((CHUNK, d), lambda b, c: (b * nc + c, 0)),
                  pl.BlockSpec((CHUNK, DT_PAD), lambda b, c: (b * nc + c, 0)),
                  const((SSD_CONV_WIDTH, SSD_CONV_DIM)),
                  const((1, SSD_CONV_DIM)),
                  const((1, DT_PAD)),
                  const((1, DT_PAD)),
                  const((1, d)),
                  const((1, d)),
                  const((DT_PAD, d))],
        out_specs=pl.BlockSpec((CHUNK, d), lambda b, c: (b * nc + c, 0)),
        out_shape=jax.ShapeDtypeStruct((t, d), BF16),
        scratch_shapes=[pltpu.VMEM((8 + CHUNK, SSD_CONV_DIM), F32),
                        pltpu.VMEM((SSD_D_STATE, d), F32)],
        compiler_params=_params("parallel", "arbitrary"),
        name="ssd",
    )(xbc, z, dt_raw, conv_w.astype(F32), conv_b.astype(F32).reshape(1, SSD_CONV_DIM), per_head(dt_bias),
      per_head(a_log), per_channel(d_skip), ssd_norm.astype(F32).reshape(1, d), expand)


def _sb_kernel(q_ref, k_ref, v_ref, o_ref, *, tb, n_heads_per_step, n_fused):
    hd = SB_HEAD_DIM
    scale2 = hd ** -0.5 * math.log2(math.e)
    nq = q_ref.shape[0] // tb
    heads = range(n_heads_per_step)
    row = lax.broadcasted_iota(jnp.int32, (tb, tb), 0)
    col = lax.broadcasted_iota(jnp.int32, (tb, tb), 1)
    upper = (row > col).astype(BF16)
    upper2 = jnp.concatenate([upper, upper], axis=0)
    before = col < row

    cols = [slice(h * hd, (h + 1) * hd) for h in heads]

    def key_blocks(i, blocks, accs, rs):
        q0 = pl.multiple_of(i * tb, tb)
        k0s = [pl.multiple_of(j * tb, tb) for j, _ in blocks]
        units = [(t, h) for t in range(len(blocks)) for h in heads]
        zs = [_dot_nt(q_ref[pl.ds(q0, tb), cols[h]], k_ref[pl.ds(k0s[t], tb), cols[h]]) * scale2 for t, h in units]
        log_betas = [jnp.minimum(z, 0.0) - jnp.log2(1.0 + jnp.exp2(-jnp.abs(z))) for z in zs]
        log_keeps = [lb - z for lb, z in zip(log_betas, zs)]
        log_keeps = [jnp.where(before, lk, 0.0) if blocks[t][1] else lk for lk, (t, _) in zip(log_keeps, units)]
        his = [lk.astype(BF16) for lk in log_keeps]
        los = [(lk - hi.astype(F32)).astype(BF16) for lk, hi in zip(log_keeps, his)]
        suffixes = [_dot(jnp.concatenate([hi, lo], axis=1), upper2) for hi, lo in zip(his, los)]
        partial = [lb + sf for lb, sf in zip(log_betas, suffixes)]
        row_sums = [jnp.sum(lk, axis=-1, keepdims=True) for lk in log_keeps]
        accs, rs = list(accs), list(rs)
        for u, (t, h) in enumerate(units):
            w = jnp.exp2(partial[u] + rs[h])
            if blocks[t][1]:
                w = jnp.where(before, w, 0.0)
            accs[h] = accs[h] + _dot(w.astype(BF16), v_ref[pl.ds(k0s[t], tb), cols[h]])
            rs[h] = rs[h] + row_sums[u]
        return tuple(accs), tuple(rs)

    def still_live(rs):
        worst = functools.reduce(jnp.maximum, rs)
        return (jnp.max(worst) > SB_EXP2_UNDERFLOW).astype(jnp.int32)

    def q_block(i, n_first):
        i = jnp.asarray(i, jnp.int32)
        accs = tuple(jnp.zeros((tb, hd), F32) for _ in heads)
        rs = tuple(jnp.zeros((tb, 1), F32) for _ in heads)
        accs, rs = key_blocks(i, [(i - t, t == 0) for t in range(n_first)], accs, rs)

        def cond(c):
            return jnp.logical_and(c[0] >= 0, c[1] > 0)

        def body(c):
            accs, rs = key_blocks(i, [(c[0], False)], c[2], c[3])
            return c[0] - 1, still_live(rs), accs, rs

        _, _, accs, rs = lax.while_loop(cond, body, (i - n_first, still_live(rs), accs, rs))
        q0 = pl.multiple_of(i * tb, tb)
        for h in heads:
            o_ref[pl.ds(q0, tb), cols[h]] = accs[h].astype(o_ref.dtype)

    for i in range(min(n_fused - 1, nq)):
        q_block(i, i + 1)

    def steady(i, carry):
        q_block(i, n_fused)
        return carry

    lax.fori_loop(n_fused - 1, nq, steady, 0)


def _sb_attention(qkv, batch, seq, n_heads, tb=128, n_heads_per_step=4, n_fused=3):
    t = batch * seq
    hw = n_heads_per_step * SB_HEAD_DIM
    ng = n_heads // n_heads_per_step
    return pl.pallas_call(
        functools.partial(_sb_kernel, tb=tb, n_heads_per_step=n_heads_per_step, n_fused=n_fused),
        grid=(batch, ng),
        in_specs=[pl.BlockSpec((seq, hw), lambda b, h: (b, h)),
                  pl.BlockSpec((seq, hw), lambda b, h: (b, ng + h)),
                  pl.BlockSpec((seq, hw), lambda b, h: (b, 2 * ng + h))],
        out_specs=pl.BlockSpec((seq, hw), lambda b, h: (b, h)),
        out_shape=jax.ShapeDtypeStruct((t, n_heads * SB_HEAD_DIM), BF16),
        compiler_params=_params("parallel", "parallel"),
        name="sb_attention",
    )(qkv, qkv, qkv)


def _xattn_kernel(q_ref, kv_ref, o_ref, *, n_heads, head_dim):
    scale = head_dim ** -0.5
    d = n_heads * head_dim
    for h in range(n_heads):
        c0 = h * head_dim
        s = _dot_nt(q_ref[:, c0:c0 + head_dim], kv_ref[:, c0:c0 + head_dim]) * scale
        p = jnp.exp(s - jnp.max(s, axis=-1, keepdims=True))
        p = p / jnp.sum(p, axis=-1, keepdims=True)
        o_ref[:, c0:c0 + head_dim] = _dot(p.astype(BF16), kv_ref[:, d + c0:d + c0 + head_dim]).astype(o_ref.dtype)


def _xattn(q, kv, batch, seq, n_mem, n_heads, tq=512):
    t, d = q.shape
    nq = seq // tq
    return pl.pallas_call(
        functools.partial(_xattn_kernel, n_heads=n_heads, head_dim=d // n_heads),
        grid=(batch, nq),
        in_specs=[pl.BlockSpec((tq, d), lambda b, i: (b * nq + i, 0)),
                  pl.BlockSpec((n_mem, 2 * d), lambda b, i: (b, 0))],
        out_specs=pl.BlockSpec((tq, d), lambda b, i: (b * nq + i, 0)),
        out_shape=jax.ShapeDtypeStruct((t, d), BF16),
        compiler_params=_params("parallel", "arbitrary"),
        name="xattn",
    )(q, kv)


def _layer(x, mem, norm_mix, w_in, b_gate, conv_w, conv_b, dt_bias, a_log, d_skip, ssd_norm,
           w_ssd_out, w_sb_out, w_out, norm_xa, norm_mem, w_xa_q, w_xa_kv, w_xa_o,
           norm_ffn, w_ffn_in, w_ffn_out, batch, seq):
    d = x.shape[1]
    n_mem = mem.shape[0] // batch
    d_sb = w_sb_out.shape[0]
    n_sb_heads = d_sb // SB_HEAD_DIM
    d_ff = w_ffn_out.shape[0]

    o_dt = SSD_D_INNER + SSD_CONV_DIM
    o_q = o_dt + SSD_N_HEADS
    o_gate = o_q + 3 * d_sb
    w_in_b = w_in.astype(BF16)
    w_dt = jnp.pad(w_in_b[:, o_dt:o_q], ((0, 0), (0, DT_PAD - SSD_N_HEADS)))
    w_qkv = w_in_b[:, o_q:o_gate]
    w_gate = w_in_b[:, o_gate:]

    h = _rmsnorm(x, norm_mix, BF16)
    z = _matmul(h, w_in_b, F32, 1024, 1024, name="proj_z", b_cols=(0, SSD_D_INNER))
    xbc = _matmul(h, w_in_b, F32, 1024, 1024, name="proj_xbc", b_cols=(SSD_D_INNER, SSD_CONV_DIM))
    dt_raw = _matmul(h, w_dt, F32, 1024, DT_PAD, name="proj_dt")
    qkv = _matmul(h, w_qkv, BF16, 1024, 1024, name="proj_qkv")
    gate = _matmul(h, w_gate, F32, 1024, 1024, name="proj_gate")

    y_ssd = _ssd(z, xbc, dt_raw, conv_w, conv_b, dt_bias, a_log, d_skip, ssd_norm, batch, seq)
    o_sb = _sb_attention(qkv, batch, seq, n_sb_heads)

    merged = _merge(y_ssd, o_sb, w_ssd_out.astype(BF16), w_sb_out.astype(BF16), gate,
                    b_gate.astype(F32).reshape(1, 2 * d))
    x = _matmul(merged, w_out.astype(BF16), F32, 1024, 512, residual=x, name="out_proj")

    hq = _rmsnorm(x, norm_xa, BF16)
    hm = _rmsnorm(mem, norm_mem, BF16)
    q = _matmul(hq, w_xa_q.astype(BF16), BF16, 1024, 1024, name="xa_q")
    kv = _matmul(hm, w_xa_kv.astype(BF16), BF16, mem.shape[0], 1024, name="xa_kv")
    o = _xattn(q, kv, batch, seq, n_mem, XA_N_HEADS)
    x = _matmul(o, w_xa_o.astype(BF16), F32, 1024, 512, residual=x, name="xa_o")

    hf = _rmsnorm(x, norm_ffn, BF16)
    act = _swiglu(hf, w_ffn_in.astype(BF16))
    x = _matmul_ksplit_res(act, w_ffn_out.astype(BF16), x, 1024, 512, d_ff // 2, name="ffn_out")
    return x


def kernel(x, mem, norm_mix, w_in, b_gate, conv_w, conv_b, dt_bias, a_log, d_skip, ssd_norm, w_ssd_out, w_sb_out, w_out, norm_xa, norm_mem, w_xa_q, w_xa_kv, w_xa_o, norm_ffn, w_ffn_in, w_ffn_out, norm_final):
    batch, seq, d = x.shape
    xf = x.reshape(batch * seq, d)
    memf = mem.reshape(-1, d)
    for l in range(norm_mix.shape[0]):
        xf = _layer(xf, memf, norm_mix[l], w_in[l], b_gate[l], conv_w[l], conv_b[l], dt_bias[l], a_log[l],
                    d_skip[l], ssd_norm[l], w_ssd_out[l], w_sb_out[l], w_out[l], norm_xa[l], norm_mem[l],
                    w_xa_q[l], w_xa_kv[l], w_xa_o[l], norm_ffn[l], w_ffn_in[l], w_ffn_out[l], batch, seq)
    out = _rmsnorm(xf, norm_final, x.dtype)
    return out.reshape(batch, seq, d)
```

```python
import functools
import math

import jax
import jax.numpy as jnp
from jax import lax
from jax.experimental import pallas as pl
from jax.experimental.pallas import tpu as pltpu

F32 = jnp.float32
BF16 = jnp.bfloat16
EPS = 1e-6

V7X_VMEM_LIMIT_BYTES = 56 * 1024 * 1024
LANES = 128

CHUNK = 64
SSD_HEAD_DIM = 64
SSD_N_HEADS = 64
SSD_N_GROUPS = 8
SSD_D_STATE = 128
SSD_D_INNER = SSD_N_HEADS * SSD_HEAD_DIM
SSD_GROUP_WIDTH = SSD_D_INNER // SSD_N_GROUPS
SSD_CONV_WIDTH = 4
SSD_CONV_DIM = SSD_D_INNER + 2 * SSD_N_GROUPS * SSD_D_STATE
DT_PAD = 128
SB_HEAD_DIM = 128
SB_EXP2_UNDERFLOW = -160.0
XA_N_HEADS = 4


def _params(*semantics):
    return pltpu.CompilerParams(dimension_semantics=semantics,
                                vmem_limit_bytes=V7X_VMEM_LIMIT_BYTES)


def _dot(a, b):
    return jnp.dot(a, b, preferred_element_type=F32)


def _dot_nt(a, b):
    return lax.dot_general(a, b, (((1,), (1,)), ((), ())), preferred_element_type=F32)


def _sigmoid(x):
    return 0.5 + 0.5 * jnp.tanh(0.5 * x)


def _silu(x):
    half = 0.5 * x
    return half + half * jnp.tanh(half)


def _split3(v):
    p1 = v.astype(BF16)
    r1 = v - p1.astype(F32)
    p2 = r1.astype(BF16)
    p3 = (r1 - p2.astype(F32)).astype(BF16)
    return p1, p2, p3


def _rmsnorm_kernel(x_ref, w_ref, o_ref):
    x = x_ref[...]
    ms = jnp.mean(x * x, axis=-1, keepdims=True)
    o_ref[...] = (x * lax.rsqrt(ms + EPS) * w_ref[...]).astype(o_ref.dtype)


def _rmsnorm(x, w, out_dtype, tm=256):
    t, d = x.shape
    return pl.pallas_call(
        _rmsnorm_kernel,
        grid=(t // tm,),
        in_specs=[pl.BlockSpec((tm, d), lambda i: (i, 0)),
                  pl.BlockSpec((1, d), lambda i: (0, 0))],
        out_specs=pl.BlockSpec((tm, d), lambda i: (i, 0)),
        out_shape=jax.ShapeDtypeStruct((t, d), out_dtype),
        compiler_params=_params("parallel"),
        name="rmsnorm",
    )(x, w.reshape(1, d))


def _mm_kernel(a_ref, b_ref, o_ref):
    o_ref[...] = _dot(a_ref[...], b_ref[...]).astype(o_ref.dtype)


def _gained(a_ref, gain_ref):
    return (a_ref[...].astype(F32) * gain_ref[...]).astype(BF16)


def _mm_rmsnorm_kernel(a_ref, b_ref, ssq_ref, gain_ref, o_ref, *, inv_width):
    row_scale = lax.rsqrt(ssq_ref[:, 0:1] * inv_width + EPS)
    o_ref[...] = (_dot(_gained(a_ref, gain_ref), b_ref[...]) * row_scale).astype(o_ref.dtype)


def _matmul(a, b, out_dtype, tm, tn, name="matmul", b_cols=None, rmsnorm=None):
    m, k = a.shape
    col0, n = b_cols if b_cols is not None else (0, b.shape[1])
    assert col0 % tn == 0 and n % tn == 0 and m % tm == 0
    jb = col0 // tn
    in_specs = [pl.BlockSpec((tm, k), lambda i, j: (i, 0)),
                pl.BlockSpec((k, tn), lambda i, j: (0, j + jb))]
    args = [a, b]
    kern = _mm_kernel
    if rmsnorm is not None:
        row_ssq, gain = rmsnorm
        in_specs += [pl.BlockSpec((tm, LANES), lambda i, j: (i, 0)),
                     pl.BlockSpec((1, k), lambda i, j: (0, 0))]
        args += [row_ssq, gain.astype(F32).reshape(1, k)]
        kern = functools.partial(_mm_rmsnorm_kernel, inv_width=1.0 / k)
    return pl.pallas_call(
        kern,
        grid=(m // tm, n // tn),
        in_specs=in_specs,
        out_specs=pl.BlockSpec((tm, tn), lambda i, j: (i, j)),
        out_shape=jax.ShapeDtypeStruct((m, n), out_dtype),
        compiler_params=_params("parallel", "arbitrary"),
        name=name,
    )(*args)


def _mm_res_stats_kernel(a_ref, b_ref, r_ref, o_ref, ob_ref, ssq_ref):
    x = r_ref[...] + _dot(a_ref[...], b_ref[...])
    o_ref[...] = x
    ob_ref[...] = x.astype(ob_ref.dtype)

    @pl.when(pl.program_id(1) == 0)
    def _():
        ssq_ref[...] = jnp.zeros_like(ssq_ref)
    ssq_ref[...] += jnp.broadcast_to(jnp.sum(x * x, axis=-1, keepdims=True), ssq_ref.shape)


def _matmul_res_stats(a, b, residual, tm, tn, name):
    m, k = a.shape
    n = b.shape[1]
    assert n % tn == 0 and m % tm == 0
    return pl.pallas_call(
        _mm_res_stats_kernel,
        grid=(m // tm, n // tn),
        in_specs=[pl.BlockSpec((tm, k), lambda i, j: (i, 0)),
                  pl.BlockSpec((k, tn), lambda i, j: (0, j)),
                  pl.BlockSpec((tm, tn), lambda i, j: (i, j))],
        out_specs=[pl.BlockSpec((tm, tn), lambda i, j: (i, j)),
                   pl.BlockSpec((tm, tn), lambda i, j: (i, j)),
                   pl.BlockSpec((tm, LANES), lambda i, j: (i, 0))],
        out_shape=[jax.ShapeDtypeStruct((m, n), F32),
                   jax.ShapeDtypeStruct((m, n), BF16),
                   jax.ShapeDtypeStruct((m, LANES), F32)],
        compiler_params=_params("parallel", "arbitrary"),
        name=name,
    )(a, b, residual)


def _mm_ksplit_res_kernel(a_ref, b_ref, r_ref, o_ref):
    @pl.when(pl.program_id(2) == 0)
    def _():
        o_ref[...] = r_ref[...]
    o_ref[...] += _dot(a_ref[...], b_ref[...])


def _matmul_ksplit_res(a, b, residual, tm, tn, tk, name="matmul_ksplit"):
    m, k = a.shape
    n = b.shape[1]
    return pl.pallas_call(
        _mm_ksplit_res_kernel,
        grid=(m // tm, n // tn, k // tk),
        in_specs=[pl.BlockSpec((tm, tk), lambda i, j, kk: (i, kk)),
                  pl.BlockSpec((tk, tn), lambda i, j, kk: (kk, j)),
                  pl.BlockSpec((tm, tn), lambda i, j, kk: (i, j))],
        out_specs=pl.BlockSpec((tm, tn), lambda i, j, kk: (i, j)),
        out_shape=jax.ShapeDtypeStruct((m, n), F32),
        compiler_params=_params("parallel", "arbitrary", "arbitrary"),
        name=name,
    )(a, b, residual)


def _merge_kernel(y_ref, o_ref, ws_ref, wb_ref, g1_ref, g2_ref, b1_ref, b2_ref, out_ref):
    branch_ssd = _dot(y_ref[...], ws_ref[...])
    branch_sb = _dot(o_ref[...], wb_ref[...])
    g_ssd = _sigmoid(g1_ref[...] + b1_ref[...])
    g_sb = _sigmoid(g2_ref[...] + b2_ref[...])
    out_ref[...] = (g_ssd * branch_ssd + g_sb * branch_sb).astype(out_ref.dtype)


def _merge(y, o, ws, wb, gate, b_gate, tm=512, tn=512):
    t, d = y.shape
    nj = d // tn
    return pl.pallas_call(
        _merge_kernel,
        grid=(t // tm, nj),
        in_specs=[pl.BlockSpec((tm, d), lambda i, j: (i, 0)),
                  pl.BlockSpec((tm, d), lambda i, j: (i, 0)),
                  pl.BlockSpec((d, tn), lambda i, j: (0, j)),
                  pl.BlockSpec((d, tn), lambda i, j: (0, j)),
                  pl.BlockSpec((tm, tn), lambda i, j: (i, j)),
                  pl.BlockSpec((tm, tn), lambda i, j: (i, j + nj)),
                  pl.BlockSpec((1, tn), lambda i, j: (0, j)),
                  pl.BlockSpec((1, tn), lambda i, j: (0, j + nj))],
        out_specs=pl.BlockSpec((tm, tn), lambda i, j: (i, j)),
        out_shape=jax.ShapeDtypeStruct((t, d), BF16),
        compiler_params=_params("parallel", "arbitrary"),
        name="merge",
    )(y, o, ws, wb, gate, gate, b_gate, b_gate)


def _swiglu_kernel(x_ref, wg_ref, wu_ref, ssq_ref, gain_ref, o_ref, *, inv_width):
    h = _gained(x_ref, gain_ref)
    row_scale = lax.rsqrt(ssq_ref[:, 0:1] * inv_width + EPS)
    g = _dot(h, wg_ref[...]) * row_scale
    u = _dot(h, wu_ref[...]) * row_scale
    o_ref[...] = (_silu(g) * u).astype(o_ref.dtype)


def _swiglu(x, w, row_ssq, gain, tm=1024, tn=256):
    t, d = x.shape
    f = w.shape[1] // 2
    assert f % tn == 0 and t % tm == 0
    nj = f // tn
    return pl.pallas_call(
        functools.partial(_swiglu_kernel, inv_width=1.0 / d),
        grid=(t // tm, nj),
        in_specs=[pl.BlockSpec((tm, d), lambda i, j: (i, 0)),
                  pl.BlockSpec((d, tn), lambda i, j: (0, j)),
                  pl.BlockSpec((d, tn), lambda i, j: (0, j + nj)),
                  pl.BlockSpec((tm, LANES), lambda i, j: (i, 0)),
                  pl.BlockSpec((1, d), lambda i, j: (0, 0))],
        out_specs=pl.BlockSpec((tm, tn), lambda i, j: (i, j)),
        out_shape=jax.ShapeDtypeStruct((t, f), BF16),
        compiler_params=_params("parallel", "arbitrary"),
        name="swiglu",
    )(x, w, w, row_ssq, gain.astype(F32).reshape(1, d))


def _ssd_kernel(xbc_ref, z_ref, dt_ref, cw_ref, cb_ref, dtb_ref, alog_ref, dskip_ref, nw_ref, e_ref,
                h_ref, wg_ref, y_ref, gate_ref, ext_ref, state_ref):
    L = CHUNK
    GW = SSD_GROUP_WIDTH
    N = SSD_D_STATE

    @pl.when(pl.program_id(1) == 0)
    def _():
        state_ref[...] = jnp.zeros_like(state_ref)
        ext_ref[0:8, :] = jnp.zeros((8, SSD_CONV_DIM), F32)

    gate_cols = 256
    groups_per_slice = SSD_N_GROUPS * gate_cols // gate_ref.shape[1]

    def gate_slice(g):
        if g % groups_per_slice == 0:
            s = g // groups_per_slice
            c = slice(s * gate_cols, (s + 1) * gate_cols)
            gate_ref[:, c] = _dot(h_ref[...], wg_ref[:, c])

    ext_ref[8:8 + L, :] = xbc_ref[...]

    def conv_silu(c0, width):
        acc = cb_ref[:, c0:c0 + width]
        for k in range(SSD_CONV_WIDTH):
            r0 = 8 - (SSD_CONV_WIDTH - 1) + k
            acc = acc + cw_ref[k:k + 1, c0:c0 + width] * ext_ref[r0:r0 + L, c0:c0 + width]
        return _silu(acc)

    tl = lax.broadcasted_iota(jnp.int32, (L, L), 0)
    ts = lax.broadcasted_iota(jnp.int32, (L, L), 1)
    tril = (tl >= ts).astype(BF16)
    tril3 = jnp.concatenate([tril, tril, tril], axis=1)

    dt_in = dt_ref[...] + dtb_ref[...]
    dt = jnp.maximum(dt_in, 0.0) + jnp.log1p(jnp.exp(-jnp.abs(dt_in)))
    a_step = dt * (-jnp.exp(alog_ref[...]))
    a_cum_h = _dot(tril3, jnp.concatenate(_split3(a_step), axis=0))
    head_pieces = jnp.concatenate(_split3(dt) + _split3(a_cum_h), axis=0)

    row = lax.broadcasted_iota(jnp.int32, (L, GW), 0)
    col = lax.broadcasted_iota(jnp.int32, (L, GW), 1)
    col_in_head = col & (SSD_HEAD_DIM - 1)
    causal = row >= col_in_head
    diag = row == col_in_head
    head4 = lax.broadcasted_iota(jnp.int32, (L, 4 * SSD_HEAD_DIM), 1) >> 6
    head_masks = [(head4 == r).astype(BF16) for r in range(4)]

    for g in range(SSD_N_GROUPS):
        gate_slice(g)
        c0 = g * GW
        xs = conv_silu(c0, GW)
        bm = conv_silu(SSD_D_INNER + g * N, N)
        cm = conv_silu(SSD_D_INNER + SSD_N_GROUPS * N + g * N, N)

        ex = _dot(head_pieces, e_ref[:, c0:c0 + GW])
        dt_x = ex[0:L] + ex[L:2 * L] + ex[2 * L:3 * L]
        a_cum = ex[3 * L:4 * L] + ex[4 * L:5 * L] + ex[5 * L:6 * L]
        a_last = a_cum[L - 1:L, :]
        exp_a = jnp.exp(a_cum)
        decay_to_end = jnp.exp(a_last - a_cum)

        xd = xs * dt_x

        a_at_s = jnp.sum(jnp.where(diag, a_cum, 0.0), axis=0, keepdims=True)
        lmat = jnp.where(causal, jnp.exp(jnp.minimum(a_cum - a_at_s, 0.0)), 0.0)
        bm_b = bm.astype(BF16)
        cm_b = cm.astype(BF16)
        b_tiled = jnp.concatenate([bm_b] * (GW // L), axis=0)
        cb = _dot_nt(cm_b, b_tiled)
        m_b = (cb * lmat).astype(BF16)

        xd_b = xd.astype(BF16)
        y_parts = []
        for q in range(GW // (4 * SSD_HEAD_DIM)):
            q0 = q * 4 * SSD_HEAD_DIM
            x4 = xd_b[:, q0:q0 + 4 * SSD_HEAD_DIM]
            blockdiag = jnp.concatenate([x4 * mask for mask in head_masks], axis=0)
            y_parts.append(_dot(m_b[:, q0:q0 + 4 * SSD_HEAD_DIM], blockdiag))
        y_diag = jnp.concatenate(y_parts, axis=1)

        prev = state_ref[:, c0:c0 + GW]
        y_off = _dot(cm_b, prev.astype(BF16)) * exp_a
        chunk_state = _dot(bm.T.astype(BF16), (xd * decay_to_end).astype(BF16))
        state_ref[:, c0:c0 + GW] = prev * jnp.exp(a_last) + chunk_state

        y = y_diag + y_off + dskip_ref[:, c0:c0 + GW] * xs

        gated = y * _silu(z_ref[:, c0:c0 + GW])
        ms = jnp.mean(gated * gated, axis=-1, keepdims=True)
        y_ref[:, c0:c0 + GW] = (gated * lax.rsqrt(ms + EPS) * nw_ref[:, c0:c0 + GW]).astype(y_ref.dtype)

    ext_ref[0:8, :] = xbc_ref[L - 8:L, :]


def _ssd(z, xbc, dt_raw, conv_w, conv_b, dt_bias, a_log, d_skip, ssd_norm, h, w_gate, batch, seq, gate_tn=1024):
    t = batch * seq
    nc = seq // CHUNK
    d = SSD_D_INNER
    k_gate, n_gate = w_gate.shape
    gate_nj = n_gate // gate_tn
    gate_tm = t * gate_nj // (batch * nc)
    assert n_gate % gate_tn == 0 and (batch * nc) % gate_nj == 0 and t % gate_tm == 0
    head_of_channel = jnp.arange(d, dtype=jnp.int32) // SSD_HEAD_DIM
    expand = (jnp.arange(DT_PAD, dtype=jnp.int32)[:, None] == head_of_channel[None, :]).astype(BF16)
    per_channel = lambda v: jnp.repeat(v.astype(F32), SSD_HEAD_DIM).reshape(1, d)
    per_head = lambda v: jnp.pad(v.astype(F32), (0, DT_PAD - SSD_N_HEADS)).reshape(1, DT_PAD)
    const = lambda shape: pl.BlockSpec(shape, lambda b, c: (0, 0))
    return pl.pallas_call(
        _ssd_kernel,
        grid=(batch, nc),
        in_specs=[pl.BlockSpec((CHUNK, SSD_CONV_DIM), lambda b, c: (b * nc + c, 0)),
                  pl.BlockSpec((CHUNK, d), lambda b, c: (b * nc + c, 0)),
                  pl.BlockSpec((CHUNK, DT_PAD), lambda b, c: (b * nc + c, 0)),
                  const((SSD_CONV_WIDTH, SSD_CONV_DIM)),
                  const((1, SSD_CONV_DIM)),
                  const((1, DT_PAD)),
                  const((1, DT_PAD)),
                  const((1, d)),
                  const((1, d)),
                  const((DT_PAD, d)),
                  pl.BlockSpec((gate_tm, k_gate), lambda b, c: ((b * nc + c) // gate_nj, 0)),
                  pl.BlockSpec((k_gate, gate_tn), lambda b, c: (0, (b * nc + c) % gate_nj))],
        out_specs=[pl.BlockSpec((CHUNK, d), lambda b, c: (b * nc + c, 0)),
                   pl.BlockSpec((gate_tm, gate_tn), lambda b, c: ((b * nc + c) // gate_nj, (b * nc + c) % gate_nj))],
        out_shape=[jax.ShapeDtypeStruct((t, d), BF16),
                   jax.ShapeDtypeStruct((t, n_gate), F32)],
        scratch_shapes=[pltpu.VMEM((8 + CHUNK, SSD_CONV_DIM), F32),
                        pltpu.VMEM((SSD_D_STATE, d), F32)],
        compiler_params=_params("arbitrary", "arbitrary"),
        name="ssd",
    )(xbc, z, dt_raw, conv_w.astype(F32), conv_b.astype(F32).reshape(1, SSD_CONV_DIM), per_head(dt_bias),
      per_head(a_log), per_channel(d_skip), ssd_norm.astype(F32).reshape(1, d), expand, h, w_gate)


def _sb_kernel(q_ref, k_ref, v_ref, o_ref, *, tb, n_heads_per_step, n_fused):
    hd = SB_HEAD_DIM
    scale2 = hd ** -0.5 * math.log2(math.e)
    nq = q_ref.shape[0] // tb
    heads = range(n_heads_per_step)
    row = lax.broadcasted_iota(jnp.int32, (tb, tb), 0)
    col = lax.broadcasted_iota(jnp.int32, (tb, tb), 1)
    upper = (row > col).astype(BF16)
    upper2 = jnp.concatenate([upper, upper], axis=0)
    before = col < row

    cols = [slice(h * hd, (h + 1) * hd) for h in heads]

    def key_blocks(i, blocks, accs, rs):
        q0 = pl.multiple_of(i * tb, tb)
        k0s = [pl.multiple_of(j * tb, tb) for j, _ in blocks]
        units = [(t, h) for t in range(len(blocks)) for h in heads]
        zs = [_dot_nt(q_ref[pl.ds(q0, tb), cols[h]], k_ref[pl.ds(k0s[t], tb), cols[h]]) * scale2 for t, h in units]
        log_betas = [jnp.minimum(z, 0.0) - jnp.log2(1.0 + jnp.exp2(-jnp.abs(z))) for z in zs]
        log_keeps = [lb - z for lb, z in zip(log_betas, zs)]
        log_keeps = [jnp.where(before, lk, 0.0) if blocks[t][1] else lk for lk, (t, _) in zip(log_keeps, units)]
        his = [lk.astype(BF16) for lk in log_keeps]
        los = [(lk - hi.astype(F32)).astype(BF16) for lk, hi in zip(log_keeps, his)]
        suffixes = [_dot(jnp.concatenate([hi, lo], axis=1), upper2) for hi, lo in zip(his, los)]
        partial = [lb + sf for lb, sf in zip(log_betas, suffixes)]
        row_sums = [jnp.sum(lk, axis=-1, keepdims=True) for lk in log_keeps]
        accs, rs = list(accs), list(rs)
        for u, (t, h) in enumerate(units):
            w = jnp.exp2(partial[u] + rs[h])
            if blocks[t][1]:
                w = jnp.where(before, w, 0.0)
            accs[h] = accs[h] + _dot(w.astype(BF16), v_ref[pl.ds(k0s[t], tb), cols[h]])
            rs[h] = rs[h] + row_sums[u]
        return tuple(accs), tuple(rs)

    def still_live(rs):
        worst = functools.reduce(jnp.maximum, rs)
        return (jnp.max(worst) > SB_EXP2_UNDERFLOW).astype(jnp.int32)

    def q_block(i, n_first):
        i = jnp.asarray(i, jnp.int32)
        accs = tuple(jnp.zeros((tb, hd), F32) for _ in heads)
        rs = tuple(jnp.zeros((tb, 1), F32) for _ in heads)
        accs, rs = key_blocks(i, [(i - t, t == 0) for t in range(n_first)], accs, rs)

        def cond(c):
            return jnp.logical_and(c[0] >= 0, c[1] > 0)

        def body(c):
            accs, rs = key_blocks(i, [(c[0], False)], c[2], c[3])
            return c[0] - 1, still_live(rs), accs, rs

        _, _, accs, rs = lax.while_loop(cond, body, (i - n_first, still_live(rs), accs, rs))
        q0 = pl.multiple_of(i * tb, tb)
        for h in heads:
            o_ref[pl.ds(q0, tb), cols[h]] = accs[h].astype(o_ref.dtype)

    for i in range(min(n_fused - 1, nq)):
        q_block(i, i + 1)

    def steady(i, carry):
        q_block(i, n_fused)
        return carry

    lax.fori_loop(n_fused - 1, nq, steady, 0)


def _sb_attention(qkv, batch, seq, n_heads, tb=128, n_heads_per_step=4, n_fused=3):
    t = batch * seq
    hw = n_heads_per_step * SB_HEAD_DIM
    ng = n_heads // n_heads_per_step
    return pl.pallas_call(
        functools.partial(_sb_kernel, tb=tb, n_heads_per_step=n_heads_per_step, n_fused=n_fused),
        grid=(batch, ng),
        in_specs=[pl.BlockSpec((seq, hw), lambda b, h: (b, h)),
                  pl.BlockSpec((seq, hw), lambda b, h: (b, ng + h)),
                  pl.BlockSpec((seq, hw), lambda b, h: (b, 2 * ng + h))],
        out_specs=pl.BlockSpec((seq, hw), lambda b, h: (b, h)),
        out_shape=jax.ShapeDtypeStruct((t, n_heads * SB_HEAD_DIM), BF16),
        compiler_params=_params("parallel", "parallel"),
        name="sb_attention",
    )(qkv, qkv, qkv)


def _xattn_kernel(q_ref, kv_ref, o_ref, *, n_heads, head_dim):
    scale = head_dim ** -0.5
    d = n_heads * head_dim
    for h in range(n_heads):
        c0 = h * head_dim
        s = _dot_nt(q_ref[:, c0:c0 + head_dim], kv_ref[:, c0:c0 + head_dim]) * scale
        p = jnp.exp(s - jnp.max(s, axis=-1, keepdims=True))
        p = p / jnp.sum(p, axis=-1, keepdims=True)
        o_ref[:, c0:c0 + head_dim] = _dot(p.astype(BF16), kv_ref[:, d + c0:d + c0 + head_dim]).astype(o_ref.dtype)


def _xattn(q, kv, batch, seq, n_mem, n_heads, tq=512):
    t, d = q.shape
    nq = seq // tq
    return pl.pallas_call(
        functools.partial(_xattn_kernel, n_heads=n_heads, head_dim=d // n_heads),
        grid=(batch, nq),
        in_specs=[pl.BlockSpec((tq, d), lambda b, i: (b * nq + i, 0)),
                  pl.BlockSpec((n_mem, 2 * d), lambda b, i: (b, 0))],
        out_specs=pl.BlockSpec((tq, d), lambda b, i: (b * nq + i, 0)),
        out_shape=jax.ShapeDtypeStruct((t, d), BF16),
        compiler_params=_params("parallel", "arbitrary"),
        name="xattn",
    )(q, kv)


def _layer(x, mem, norm_mix, w_in, b_gate, conv_w, conv_b, dt_bias, a_log, d_skip, ssd_norm,
           w_ssd_out, w_sb_out, w_out, norm_xa, norm_mem, w_xa_q, w_xa_kv, w_xa_o,
           norm_ffn, w_ffn_in, w_ffn_out, batch, seq):
    d = x.shape[1]
    n_mem = mem.shape[0] // batch
    d_sb = w_sb_out.shape[0]
    n_sb_heads = d_sb // SB_HEAD_DIM
    d_ff = w_ffn_out.shape[0]

    o_dt = SSD_D_INNER + SSD_CONV_DIM
    o_q = o_dt + SSD_N_HEADS
    o_gate = o_q + 3 * d_sb
    w_in_b = w_in.astype(BF16)
    w_dt = jnp.pad(w_in_b[:, o_dt:o_q], ((0, 0), (0, DT_PAD - SSD_N_HEADS)))
    w_qkv = w_in_b[:, o_q:o_gate]
    w_gate = w_in_b[:, o_gate:]

    h = _rmsnorm(x, norm_mix, BF16)
    z = _matmul(h, w_in_b, F32, 1024, 1024, name="proj_z", b_cols=(0, SSD_D_INNER))
    xbc = _matmul(h, w_in_b, F32, 1024, 1024, name="proj_xbc", b_cols=(SSD_D_INNER, SSD_CONV_DIM))
    dt_raw = _matmul(h, w_dt, F32, 1024, DT_PAD, name="proj_dt")
    qkv = _matmul(h, w_qkv, BF16, 1024, 1024, name="proj_qkv")

    y_ssd, gate = _ssd(z, xbc, dt_raw, conv_w, conv_b, dt_bias, a_log, d_skip, ssd_norm, h, w_gate, batch, seq)
    o_sb = _sb_attention(qkv, batch, seq, n_sb_heads)

    merged = _merge(y_ssd, o_sb, w_ssd_out.astype(BF16), w_sb_out.astype(BF16), gate,
                    b_gate.astype(F32).reshape(1, 2 * d))
    x, x_b, x_ssq = _matmul_res_stats(merged, w_out.astype(BF16), x, 1024, 512, name="out_proj")

    hm = _rmsnorm(mem, norm_mem, BF16)
    q = _matmul(x_b, w_xa_q.astype(BF16), BF16, 1024, 1024, name="xa_q", rmsnorm=(x_ssq, norm_xa))
    kv = _matmul(hm, w_xa_kv.astype(BF16), BF16, mem.shape[0], 1024, name="xa_kv")
    o = _xattn(q, kv, batch, seq, n_mem, XA_N_HEADS)
    x, x_b, x_ssq = _matmul_res_stats(o, w_xa_o.astype(BF16), x, 1024, 512, name="xa_o")

    act = _swiglu(x_b, w_ffn_in.astype(BF16), x_ssq, norm_ffn)
    x = _matmul_ksplit_res(act, w_ffn_out.astype(BF16), x, 1024, 512, d_ff // 2, name="ffn_out")
    return x


def kernel(x, mem, norm_mix, w_in, b_gate, conv_w, conv_b, dt_bias, a_log, d_skip, ssd_norm, w_ssd_out, w_sb_out, w_out, norm_xa, norm_mem, w_xa_q, w_xa_kv, w_xa_o, norm_ffn, w_ffn_in, w_ffn_out, norm_final):
    batch, seq, d = x.shape
    xf = x.reshape(batch * seq, d)
    memf = mem.reshape(-1, d)
    for l in range(norm_mix.shape[0]):
        xf = _layer(xf, memf, norm_mix[l], w_in[l], b_gate[l], conv_w[l], conv_b[l], dt_bias[l], a_log[l],
                    d_skip[l], ssd_norm[l], w_ssd_out[l], w_sb_out[l], w_out[l], norm_xa[l], norm_mem[l],
                    w_xa_q[l], w_xa_kv[l], w_xa_o[l], norm_ffn[l], w_ffn_in[l], w_ffn_out[l], batch, seq)
    out = _rmsnorm(xf, norm_final, x.dtype)
    return out.reshape(batch, seq, d)
```

```python
import functools
import math

import jax
import jax.numpy as jnp
from jax import lax
from jax.experimental import pallas as pl
from jax.experimental.pallas import tpu as pltpu

F32 = jnp.float32
BF16 = jnp.bfloat16
EPS = 1e-6

V7X_VMEM_LIMIT_BYTES = 56 * 1024 * 1024
LANES = 128

CHUNK = 64
SSD_HEAD_DIM = 64
SSD_N_HEADS = 64
SSD_N_GROUPS = 8
SSD_D_STATE = 128
SSD_D_INNER = SSD_N_HEADS * SSD_HEAD_DIM
SSD_GROUP_WIDTH = SSD_D_INNER // SSD_N_GROUPS
SSD_CONV_WIDTH = 4
SSD_CONV_DIM = SSD_D_INNER + 2 * SSD_N_GROUPS * SSD_D_STATE
DT_PAD = 128
SB_HEAD_DIM = 128
SB_EXP2_UNDERFLOW = -160.0
XA_N_HEADS = 4


def _params(*semantics):
    return pltpu.CompilerParams(dimension_semantics=semantics,
                                vmem_limit_bytes=V7X_VMEM_LIMIT_BYTES)


def _dot(a, b):
    return jnp.dot(a, b, preferred_element_type=F32)


def _dot_nt(a, b):
    return lax.dot_general(a, b, (((1,), (1,)), ((), ())), preferred_element_type=F32)


def _sigmoid(x):
    return 0.5 + 0.5 * jnp.tanh(0.5 * x)


def _silu(x):
    half = 0.5 * x
    return half + half * jnp.tanh(half)


def _split3(v):
    p1 = v.astype(BF16)
    r1 = v - p1.astype(F32)
    p2 = r1.astype(BF16)
    p3 = (r1 - p2.astype(F32)).astype(BF16)
    return p1, p2, p3


def _rmsnorm_kernel(x_ref, w_ref, o_ref):
    x = x_ref[...]
    ms = jnp.mean(x * x, axis=-1, keepdims=True)
    o_ref[...] = (x * lax.rsqrt(ms + EPS) * w_ref[...]).astype(o_ref.dtype)


def _rmsnorm(x, w, out_dtype, tm=256):
    t, d = x.shape
    return pl.pallas_call(
        _rmsnorm_kernel,
        grid=(t // tm,),
        in_specs=[pl.BlockSpec((tm, d), lambda i: (i, 0)),
                  pl.BlockSpec((1, d), lambda i: (0, 0))],
        out_specs=pl.BlockSpec((tm, d), lambda i: (i, 0)),
        out_shape=jax.ShapeDtypeStruct((t, d), out_dtype),
        compiler_params=_params("parallel"),
        name="rmsnorm",
    )(x, w.reshape(1, d))


def _mm_kernel(a_ref, b_ref, o_ref):
    o_ref[...] = _dot(a_ref[...], b_ref[...]).astype(o_ref.dtype)


def _gained(a_ref, gain_ref):
    return (a_ref[...].astype(F32) * gain_ref[...]).astype(BF16)


def _mm_rmsnorm_kernel(a_ref, b_ref, ssq_ref, gain_ref, o_ref, *, inv_width):
    row_scale = lax.rsqrt(ssq_ref[:, 0:1] * inv_width + EPS)
    o_ref[...] = (_dot(_gained(a_ref, gain_ref), b_ref[...]) * row_scale).astype(o_ref.dtype)


def _matmul(a, b, out_dtype, tm, tn, name="matmul", b_cols=None, rmsnorm=None):
    m, k = a.shape
    col0, n = b_cols if b_cols is not None else (0, b.shape[1])
    assert col0 % tn == 0 and n % tn == 0 and m % tm == 0
    jb = col0 // tn
    in_specs = [pl.BlockSpec((tm, k), lambda i, j: (i, 0)),
                pl.BlockSpec((k, tn), lambda i, j: (0, j + jb))]
    args = [a, b]
    kern = _mm_kernel
    if rmsnorm is not None:
        row_ssq, gain = rmsnorm
        in_specs += [pl.BlockSpec((tm, LANES), lambda i, j: (i, 0)),
                     pl.BlockSpec((1, k), lambda i, j: (0, 0))]
        args += [row_ssq, gain.astype(F32).reshape(1, k)]
        kern = functools.partial(_mm_rmsnorm_kernel, inv_width=1.0 / k)
    return pl.pallas_call(
        kern,
        grid=(m // tm, n // tn),
        in_specs=in_specs,
        out_specs=pl.BlockSpec((tm, tn), lambda i, j: (i, j)),
        out_shape=jax.ShapeDtypeStruct((m, n), out_dtype),
        compiler_params=_params("parallel", "arbitrary"),
        name=name,
    )(*args)


def _mm_nt_kernel(a_ref, bt_ref, o_ref):
    o_ref[...] = _dot_nt(a_ref[...], bt_ref[...]).astype(o_ref.dtype)


def _matmul_nt(a, bt, out_dtype, tm, tn, rows, name):
    m, k = a.shape
    row0, n = rows
    assert row0 % tn == 0 and n % tn == 0 and m % tm == 0
    jb = row0 // tn
    return pl.pallas_call(
        _mm_nt_kernel,
        grid=(m // tm, n // tn),
        in_specs=[pl.BlockSpec((tm, k), lambda i, j: (i, 0)),
                  pl.BlockSpec((tn, k), lambda i, j: (j + jb, 0))],
        out_specs=pl.BlockSpec((tm, tn), lambda i, j: (i, j)),
        out_shape=jax.ShapeDtypeStruct((m, n), out_dtype),
        compiler_params=_params("parallel", "arbitrary"),
        name=name,
    )(a, bt)


def _mm_res_stats_kernel(a_ref, b_ref, r_ref, o_ref, ob_ref, ssq_ref):
    x = r_ref[...] + _dot(a_ref[...], b_ref[...])
    o_ref[...] = x
    ob_ref[...] = x.astype(ob_ref.dtype)

    @pl.when(pl.program_id(1) == 0)
    def _():
        ssq_ref[...] = jnp.zeros_like(ssq_ref)
    ssq_ref[...] += jnp.broadcast_to(jnp.sum(x * x, axis=-1, keepdims=True), ssq_ref.shape)


def _matmul_res_stats(a, b, residual, tm, tn, name):
    m, k = a.shape
    n = b.shape[1]
    assert n % tn == 0 and m % tm == 0
    return pl.pallas_call(
        _mm_res_stats_kernel,
        grid=(m // tm, n // tn),
        in_specs=[pl.BlockSpec((tm, k), lambda i, j: (i, 0)),
                  pl.BlockSpec((k, tn), lambda i, j: (0, j)),
                  pl.BlockSpec((tm, tn), lambda i, j: (i, j))],
        out_specs=[pl.BlockSpec((tm, tn), lambda i, j: (i, j)),
                   pl.BlockSpec((tm, tn), lambda i, j: (i, j)),
                   pl.BlockSpec((tm, LANES), lambda i, j: (i, 0))],
        out_shape=[jax.ShapeDtypeStruct((m, n), F32),
                   jax.ShapeDtypeStruct((m, n), BF16),
                   jax.ShapeDtypeStruct((m, LANES), F32)],
        compiler_params=_params("parallel", "arbitrary"),
        name=name,
    )(a, b, residual)


def _mm_ksplit_res_kernel(a_ref, b_ref, r_ref, o_ref):
    @pl.when(pl.program_id(2) == 0)
    def _():
        o_ref[...] = r_ref[...]
    o_ref[...] += _dot(a_ref[...], b_ref[...])


def _matmul_ksplit_res(a, b, residual, tm, tn, tk, name="matmul_ksplit"):
    m, k = a.shape
    n = b.shape[1]
    return pl.pallas_call(
        _mm_ksplit_res_kernel,
        grid=(m // tm, n // tn, k // tk),
        in_specs=[pl.BlockSpec((tm, tk), lambda i, j, kk: (i, kk)),
                  pl.BlockSpec((tk, tn), lambda i, j, kk: (kk, j)),
                  pl.BlockSpec((tm, tn), lambda i, j, kk: (i, j))],
        out_specs=pl.BlockSpec((tm, tn), lambda i, j, kk: (i, j)),
        out_shape=jax.ShapeDtypeStruct((m, n), F32),
        compiler_params=_params("parallel", "arbitrary", "arbitrary"),
        name=name,
    )(a, b, residual)


def _merge_kernel(y_ref, o_ref, ws_ref, wb_ref, g1_ref, g2_ref, b1_ref, b2_ref, out_ref):
    branch_ssd = _dot(y_ref[...], ws_ref[...])
    branch_sb = _dot(o_ref[...], wb_ref[...])
    g_ssd = _sigmoid(g1_ref[...] + b1_ref[...])
    g_sb = _sigmoid(g2_ref[...] + b2_ref[...])
    out_ref[...] = (g_ssd * branch_ssd + g_sb * branch_sb).astype(out_ref.dtype)


def _merge(y, o, ws, wb, gate, b_gate, tm=512, tn=512):
    t, d = y.shape
    nj = d // tn
    return pl.pallas_call(
        _merge_kernel,
        grid=(t // tm, nj),
        in_specs=[pl.BlockSpec((tm, d), lambda i, j: (i, 0)),
                  pl.BlockSpec((tm, d), lambda i, j: (i, 0)),
                  pl.BlockSpec((d, tn), lambda i, j: (0, j)),
                  pl.BlockSpec((d, tn), lambda i, j: (0, j)),
                  pl.BlockSpec((tm, tn), lambda i, j: (i, j)),
                  pl.BlockSpec((tm, tn), lambda i, j: (i, j + nj)),
                  pl.BlockSpec((1, tn), lambda i, j: (0, j)),
                  pl.BlockSpec((1, tn), lambda i, j: (0, j + nj))],
        out_specs=pl.BlockSpec((tm, tn), lambda i, j: (i, j)),
        out_shape=jax.ShapeDtypeStruct((t, d), BF16),
        compiler_params=_params("parallel", "arbitrary"),
        name="merge",
    )(y, o, ws, wb, gate, gate, b_gate, b_gate)


def _swiglu_kernel(x_ref, wg_ref, wu_ref, ssq_ref, gain_ref, o_ref, *, inv_width):
    h = _gained(x_ref, gain_ref)
    row_scale = lax.rsqrt(ssq_ref[:, 0:1] * inv_width + EPS)
    g = _dot(h, wg_ref[...]) * row_scale
    u = _dot(h, wu_ref[...]) * row_scale
    o_ref[...] = (_silu(g) * u).astype(o_ref.dtype)


def _swiglu(x, w, row_ssq, gain, tm=2048, tn=256):
    t, d = x.shape
    f = w.shape[1] // 2
    assert f % tn == 0 and t % tm == 0
    nj = f // tn
    return pl.pallas_call(
        functools.partial(_swiglu_kernel, inv_width=1.0 / d),
        grid=(t // tm, nj),
        in_specs=[pl.BlockSpec((tm, d), lambda i, j: (i, 0)),
                  pl.BlockSpec((d, tn), lambda i, j: (0, j)),
                  pl.BlockSpec((d, tn), lambda i, j: (0, j + nj)),
                  pl.BlockSpec((tm, LANES), lambda i, j: (i, 0)),
                  pl.BlockSpec((1, d), lambda i, j: (0, 0))],
        out_specs=pl.BlockSpec((tm, tn), lambda i, j: (i, j)),
        out_shape=jax.ShapeDtypeStruct((t, f), BF16),
        compiler_params=_params("parallel", "arbitrary"),
        name="swiglu",
    )(x, w, w, row_ssq, gain.astype(F32).reshape(1, d))


def _ssd_kernel(xbc_ref, z_ref, dt_ref, cw_ref, cb_ref, dtb_ref, alog_ref, dskip_ref, nw_ref, e_ref,
                h_ref, wg_ref, y_ref, gate_ref, ext_ref, state_ref):
    L = CHUNK
    GW = SSD_GROUP_WIDTH
    N = SSD_D_STATE

    @pl.when(pl.program_id(1) == 0)
    def _():
        state_ref[...] = jnp.zeros_like(state_ref)
        ext_ref[0:8, :] = jnp.zeros((8, SSD_CONV_DIM), F32)

    gate_cols = 256
    groups_per_slice = SSD_N_GROUPS * gate_cols // gate_ref.shape[1]

    def gate_slice(g):
        if g % groups_per_slice == 0:
            s = g // groups_per_slice
            c = slice(s * gate_cols, (s + 1) * gate_cols)
            gate_ref[:, c] = _dot_nt(h_ref[...], wg_ref[c, :])

    ext_ref[8:8 + L, :] = xbc_ref[...]

    def conv_silu(c0, width):
        acc = cb_ref[:, c0:c0 + width]
        for k in range(SSD_CONV_WIDTH):
            r0 = 8 - (SSD_CONV_WIDTH - 1) + k
            acc = acc + cw_ref[k:k + 1, c0:c0 + width] * ext_ref[r0:r0 + L, c0:c0 + width]
        return _silu(acc)

    tl = lax.broadcasted_iota(jnp.int32, (L, L), 0)
    ts = lax.broadcasted_iota(jnp.int32, (L, L), 1)
    tril = (tl >= ts).astype(BF16)
    tril3 = jnp.concatenate([tril, tril, tril], axis=1)

    dt_in = dt_ref[...] + dtb_ref[...]
    dt = jnp.maximum(dt_in, 0.0) + jnp.log1p(jnp.exp(-jnp.abs(dt_in)))
    a_step = dt * (-jnp.exp(alog_ref[...]))
    a_cum_h = _dot(tril3, jnp.concatenate(_split3(a_step), axis=0))
    head_pieces = jnp.concatenate(_split3(dt) + _split3(a_cum_h), axis=0)

    row = lax.broadcasted_iota(jnp.int32, (L, GW), 0)
    col = lax.broadcasted_iota(jnp.int32, (L, GW), 1)
    col_in_head = col & (SSD_HEAD_DIM - 1)
    causal = row >= col_in_head
    diag = row == col_in_head
    head4 = lax.broadcasted_iota(jnp.int32, (L, 4 * SSD_HEAD_DIM), 1) >> 6
    head_masks = [(head4 == r).astype(BF16) for r in range(4)]

    for g in range(SSD_N_GROUPS):
        gate_slice(g)
        c0 = g * GW
        xs = conv_silu(c0, GW)
        bm = conv_silu(SSD_D_INNER + g * N, N)
        cm = conv_silu(SSD_D_INNER + SSD_N_GROUPS * N + g * N, N)

        ex = _dot(head_pieces, e_ref[:, c0:c0 + GW])
        dt_x = ex[0:L] + ex[L:2 * L] + ex[2 * L:3 * L]
        a_cum = ex[3 * L:4 * L] + ex[4 * L:5 * L] + ex[5 * L:6 * L]
        a_last = a_cum[L - 1:L, :]
        exp_a = jnp.exp(a_cum)
        decay_to_end = jnp.exp(a_last - a_cum)

        xd = xs * dt_x

        a_at_s = jnp.sum(jnp.where(diag, a_cum, 0.0), axis=0, keepdims=True)
        lmat = jnp.where(causal, jnp.exp(jnp.minimum(a_cum - a_at_s, 0.0)), 0.0)
        bm_b = bm.astype(BF16)
        cm_b = cm.astype(BF16)
        b_tiled = jnp.concatenate([bm_b] * (GW // L), axis=0)
        cb = _dot_nt(cm_b, b_tiled)
        m_b = (cb * lmat).astype(BF16)

        xd_b = xd.astype(BF16)
        y_parts = []
        for q in range(GW // (4 * SSD_HEAD_DIM)):
            q0 = q * 4 * SSD_HEAD_DIM
            x4 = xd_b[:, q0:q0 + 4 * SSD_HEAD_DIM]
            blockdiag = jnp.concatenate([x4 * mask for mask in head_masks], axis=0)
            y_parts.append(_dot(m_b[:, q0:q0 + 4 * SSD_HEAD_DIM], blockdiag))
        y_diag = jnp.concatenate(y_parts, axis=1)

        prev = state_ref[:, c0:c0 + GW]
        y_off = _dot(cm_b, prev.astype(BF16)) * exp_a
        chunk_state = _dot(bm.T.astype(BF16), (xd * decay_to_end).astype(BF16))
        state_ref[:, c0:c0 + GW] = prev * jnp.exp(a_last) + chunk_state

        y = y_diag + y_off + dskip_ref[:, c0:c0 + GW] * xs

        gated = y * _silu(z_ref[:, c0:c0 + GW])
        ms = jnp.mean(gated * gated, axis=-1, keepdims=True)
        y_ref[:, c0:c0 + GW] = (gated * lax.rsqrt(ms + EPS) * nw_ref[:, c0:c0 + GW]).astype(y_ref.dtype)

    ext_ref[0:8, :] = xbc_ref[L - 8:L, :]


def _ssd(z, xbc, dt_raw, conv_w, conv_b, dt_bias, a_log, d_skip, ssd_norm, h, w_gate, batch, seq, gate_cols,
         gate_tn=512):
    t = batch * seq
    nc = seq // CHUNK
    d = SSD_D_INNER
    k_gate = w_gate.shape[1]
    gate_col0, n_gate = gate_cols
    gate_nj = n_gate // gate_tn
    gate_jb = gate_col0 // gate_tn
    gate_tm = t * gate_nj // (batch * nc)
    assert n_gate % gate_tn == 0 and gate_col0 % gate_tn == 0
    assert (batch * nc) % gate_nj == 0 and t % gate_tm == 0
    head_of_channel = jnp.arange(d, dtype=jnp.int32) // SSD_HEAD_DIM
    expand = (jnp.arange(DT_PAD, dtype=jnp.int32)[:, None] == head_of_channel[None, :]).astype(BF16)
    per_channel = lambda v: jnp.repeat(v.astype(F32), SSD_HEAD_DIM).reshape(1, d)
    per_head = lambda v: jnp.pad(v.astype(F32), (0, DT_PAD - SSD_N_HEADS)).reshape(1, DT_PAD)
    const = lambda shape: pl.BlockSpec(shape, lambda b, c: (0, 0))
    return pl.pallas_call(
        _ssd_kernel,
        grid=(batch, nc),
        in_specs=[pl.BlockSpec((CHUNK, SSD_CONV_DIM), lambda b, c: (b * nc + c, 0)),
                  pl.BlockSpec((CHUNK, d), lambda b, c: (b * nc + c, 0)),
                  pl.BlockSpec((CHUNK, DT_PAD), lambda b, c: (b * nc + c, 0)),
                  const((SSD_CONV_WIDTH, SSD_CONV_DIM)),
                  const((1, SSD_CONV_DIM)),
                  const((1, DT_PAD)),
                  const((1, DT_PAD)),
                  const((1, d)),
                  const((1, d)),
                  const((DT_PAD, d)),
                  pl.BlockSpec((gate_tm, k_gate), lambda b, c: ((b * nc + c) // gate_nj, 0)),
                  pl.BlockSpec((gate_tn, k_gate), lambda b, c: (gate_jb + (b * nc + c) % gate_nj, 0))],
        out_specs=[pl.BlockSpec((CHUNK, d), lambda b, c: (b * nc + c, 0)),
                   pl.BlockSpec((gate_tm, gate_tn), lambda b, c: ((b * nc + c) // gate_nj, (b * nc + c) % gate_nj))],
        out_shape=[jax.ShapeDtypeStruct((t, d), BF16),
                   jax.ShapeDtypeStruct((t, n_gate), F32)],
        scratch_shapes=[pltpu.VMEM((8 + CHUNK, SSD_CONV_DIM), F32),
                        pltpu.VMEM((SSD_D_STATE, d), F32)],
        compiler_params=_params("arbitrary", "arbitrary"),
        name="ssd",
    )(xbc, z, dt_raw, conv_w.astype(F32), conv_b.astype(F32).reshape(1, SSD_CONV_DIM), per_head(dt_bias),
      per_head(a_log), per_channel(d_skip), ssd_norm.astype(F32).reshape(1, d), expand, h, w_gate)


def _sb_kernel(q_ref, k_ref, v_ref, o_ref, *, tb, n_heads_per_step, n_fused, n_together):
    hd = SB_HEAD_DIM
    scale2 = hd ** -0.5 * math.log2(math.e)
    nq = q_ref.shape[0] // tb
    heads = range(n_heads_per_step)
    row = lax.broadcasted_iota(jnp.int32, (tb, tb), 0)
    col = lax.broadcasted_iota(jnp.int32, (tb, tb), 1)
    upper = (row > col).astype(BF16)
    upper2 = jnp.concatenate([upper, upper], axis=0)
    before = col < row

    cols = [slice(h * hd, (h + 1) * hd) for h in heads]

    def key_blocks(qis, blocks, accs, rs):
        q0s = [pl.multiple_of(i * tb, tb) for i in qis]
        k0s = [[pl.multiple_of(j * tb, tb) for j, _ in bl] for bl in blocks]
        units = [(s, t, h) for s in range(len(qis)) for t in range(len(blocks[s])) for h in heads]
        diag = [blocks[s][t][1] for s, t, _ in units]
        zs = [_dot_nt(q_ref[pl.ds(q0s[s], tb), cols[h]], k_ref[pl.ds(k0s[s][t], tb), cols[h]]) * scale2
              for s, t, h in units]
        log_betas = [jnp.minimum(z, 0.0) - jnp.log2(1.0 + jnp.exp2(-jnp.abs(z))) for z in zs]
        log_keeps = [lb - z for lb, z in zip(log_betas, zs)]
        log_keeps = [jnp.where(before, lk, 0.0) if d else lk for lk, d in zip(log_keeps, diag)]
        his = [lk.astype(BF16) for lk in log_keeps]
        los = [(lk - hi.astype(F32)).astype(BF16) for lk, hi in zip(log_keeps, his)]
        suffixes = [_dot(jnp.concatenate([hi, lo], axis=1), upper2) for hi, lo in zip(his, los)]
        partial = [lb + sf for lb, sf in zip(log_betas, suffixes)]
        row_sums = [jnp.sum(lk, axis=-1, keepdims=True) for lk in log_keeps]
        accs = [list(a) for a in accs]
        rs = [list(r) for r in rs]
        for u, (s, t, h) in enumerate(units):
            w = jnp.exp2(partial[u] + rs[s][h])
            if diag[u]:
                w = jnp.where(before, w, 0.0)
            accs[s][h] = accs[s][h] + _dot(w.astype(BF16), v_ref[pl.ds(k0s[s][t], tb), cols[h]])
            rs[s][h] = rs[s][h] + row_sums[u]
        return tuple(tuple(a) for a in accs), tuple(tuple(r) for r in rs)

    def still_live(rs):
        worst = functools.reduce(jnp.maximum, rs)
        return (jnp.max(worst) > SB_EXP2_UNDERFLOW).astype(jnp.int32)

    def q_blocks(qis, n_first):
        qis = [jnp.asarray(i, jnp.int32) for i in qis]
        accs = tuple(tuple(jnp.zeros((tb, hd), F32) for _ in heads) for _ in qis)
        rs = tuple(tuple(jnp.zeros((tb, 1), F32) for _ in heads) for _ in qis)
        accs, rs = key_blocks(qis, [[(i - t, t == 0) for t in range(n_first)] for i in qis], accs, rs)

        def cond(c):
            return jnp.logical_and(c[0] >= 0, c[1] > 0)

        for s, i in enumerate(qis):
            def body(c, i=i):
                a, r = key_blocks([i], [[(c[0], False)]], (c[2],), (c[3],))
                return c[0] - 1, still_live(r[0]), a[0], r[0]

            _, _, acc_s, _ = lax.while_loop(cond, body, (i - n_first, still_live(rs[s]), accs[s], rs[s]))
            q0 = pl.multiple_of(i * tb, tb)
            for h in heads:
                o_ref[pl.ds(q0, tb), cols[h]] = acc_s[h].astype(o_ref.dtype)

    first_steady = -(-(n_fused - 1) // n_together) * n_together
    assert (nq - first_steady) % n_together == 0
    for i in range(min(first_steady, nq)):
        q_blocks([i], min(i + 1, n_fused))

    def steady(p, carry):
        i0 = first_steady + p * n_together
        q_blocks([i0 + s for s in range(n_together)], n_fused)
        return carry

    lax.fori_loop(0, (nq - first_steady) // n_together, steady, 0)


def _sb_attention(qkv, batch, seq, n_heads, tb=128, n_heads_per_step=4, n_fused=3, n_together=1):
    t = batch * seq
    hw = n_heads_per_step * SB_HEAD_DIM
    ng = n_heads // n_heads_per_step
    return pl.pallas_call(
        functools.partial(_sb_kernel, tb=tb, n_heads_per_step=n_heads_per_step, n_fused=n_fused,
                          n_together=n_together),
        grid=(batch, ng),
        in_specs=[pl.BlockSpec((seq, hw), lambda b, h: (b, h)),
                  pl.BlockSpec((seq, hw), lambda b, h: (b, ng + h)),
                  pl.BlockSpec((seq, hw), lambda b, h: (b, 2 * ng + h))],
        out_specs=pl.BlockSpec((seq, hw), lambda b, h: (b, h)),
        out_shape=jax.ShapeDtypeStruct((t, n_heads * SB_HEAD_DIM), BF16),
        compiler_params=_params("parallel", "parallel"),
        name="sb_attention",
    )(qkv, qkv, qkv)


def _xattn_kernel(q_ref, kv_ref, o_ref, *, n_heads, head_dim):
    scale = head_dim ** -0.5
    d = n_heads * head_dim
    for h in range(n_heads):
        c0 = h * head_dim
        s = _dot_nt(q_ref[:, c0:c0 + head_dim], kv_ref[:, c0:c0 + head_dim]) * scale
        p = jnp.exp(s - jnp.max(s, axis=-1, keepdims=True))
        p = p / jnp.sum(p, axis=-1, keepdims=True)
        o_ref[:, c0:c0 + head_dim] = _dot(p.astype(BF16), kv_ref[:, d + c0:d + c0 + head_dim]).astype(o_ref.dtype)


def _xattn(q, kv, batch, seq, n_mem, n_heads, tq=512):
    t, d = q.shape
    nq = seq // tq
    return pl.pallas_call(
        functools.partial(_xattn_kernel, n_heads=n_heads, head_dim=d // n_heads),
        grid=(batch, nq),
        in_specs=[pl.BlockSpec((tq, d), lambda b, i: (b * nq + i, 0)),
                  pl.BlockSpec((n_mem, 2 * d), lambda b, i: (b, 0))],
        out_specs=pl.BlockSpec((tq, d), lambda b, i: (b * nq + i, 0)),
        out_shape=jax.ShapeDtypeStruct((t, d), BF16),
        compiler_params=_params("parallel", "arbitrary"),
        name="xattn",
    )(q, kv)


def _layer(x, mem, norm_mix, w_in, b_gate, conv_w, conv_b, dt_bias, a_log, d_skip, ssd_norm,
           w_ssd_out, w_sb_out, w_out, norm_xa, norm_mem, w_xa_q, w_xa_kv, w_xa_o,
           norm_ffn, w_ffn_in, w_ffn_out, batch, seq):
    d = x.shape[1]
    n_mem = mem.shape[0] // batch
    d_sb = w_sb_out.shape[0]
    n_sb_heads = d_sb // SB_HEAD_DIM
    d_ff = w_ffn_out.shape[0]

    o_dt = SSD_D_INNER + SSD_CONV_DIM
    o_q = o_dt + SSD_N_HEADS
    o_gate = o_q + 3 * d_sb
    w_in_t = jnp.transpose(w_in).astype(BF16)
    w_qkvg_t = w_in_t[o_q:]

    h = _rmsnorm(x, norm_mix, BF16)
    z = _matmul_nt(h, w_in_t, F32, 1024, 1024, (0, SSD_D_INNER), name="proj_z")
    xbc = _matmul_nt(h, w_in_t, F32, 1024, 1024, (SSD_D_INNER, SSD_CONV_DIM), name="proj_xbc")
    dt_raw = _matmul_nt(h, w_in_t, F32, 1024, DT_PAD, (o_dt, DT_PAD), name="proj_dt")
    qkv = _matmul_nt(h, w_qkvg_t, BF16, 1024, 1024, (0, 3 * d_sb), name="proj_qkv")

    y_ssd, gate = _ssd(z, xbc, dt_raw, conv_w, conv_b, dt_bias, a_log, d_skip, ssd_norm, h, w_qkvg_t, batch, seq,
                       gate_cols=(3 * d_sb, 2 * d))
    o_sb = _sb_attention(qkv, batch, seq, n_sb_heads)

    merged = _merge(y_ssd, o_sb, w_ssd_out.astype(BF16), w_sb_out.astype(BF16), gate,
                    b_gate.astype(F32).reshape(1, 2 * d))
    x, x_b, x_ssq = _matmul_res_stats(merged, w_out.astype(BF16), x, 1024, 512, name="out_proj")

    hm = _rmsnorm(mem, norm_mem, BF16)
    q = _matmul(x_b, w_xa_q.astype(BF16), BF16, 1024, 1024, name="xa_q", rmsnorm=(x_ssq, norm_xa))
    kv = _matmul(hm, w_xa_kv.astype(BF16), BF16, mem.shape[0], 1024, name="xa_kv")
    o = _xattn(q, kv, batch, seq, n_mem, XA_N_HEADS)
    x, x_b, x_ssq = _matmul_res_stats(o, w_xa_o.astype(BF16), x, 1024, 512, name="xa_o")

    act = _swiglu(x_b, w_ffn_in.astype(BF16), x_ssq, norm_ffn)
    x = _matmul_ksplit_res(act, w_ffn_out.astype(BF16), x, 1024, 512, d_ff // 2, name="ffn_out")
    return x


def kernel(x, mem, norm_mix, w_in, b_gate, conv_w, conv_b, dt_bias, a_log, d_skip, ssd_norm, w_ssd_out, w_sb_out, w_out, norm_xa, norm_mem, w_xa_q, w_xa_kv, w_xa_o, norm_ffn, w_ffn_in, w_ffn_out, norm_final):
    batch, seq, d = x.shape
    xf = x.reshape(batch * seq, d)
    memf = mem.reshape(-1, d)
    for l in range(norm_mix.shape[0]):
        xf = _layer(xf, memf, norm_mix[l], w_in[l], b_gate[l], conv_w[l], conv_b[l], dt_bias[l], a_log[l],
                    d_skip[l], ssd_norm[l], w_ssd_out[l], w_sb_out[l], w_out[l], norm_xa[l], norm_mem[l],
                    w_xa_q[l], w_xa_kv[l], w_xa_o[l], norm_ffn[l], w_ffn_in[l], w_ffn_out[l], batch, seq)
    out = _rmsnorm(xf, norm_final, x.dtype)
    return out.reshape(batch, seq, d)
```

```python
import functools
import math

import jax
import jax.numpy as jnp
from jax import lax
from jax.experimental import pallas as pl
from jax.experimental.pallas import tpu as pltpu

F32 = jnp.float32
BF16 = jnp.bfloat16
EPS = 1e-6

V7X_VMEM_LIMIT_BYTES = 56 * 1024 * 1024
LANES = 128

CHUNK = 64
SSD_HEAD_DIM = 64
SSD_N_HEADS = 64
SSD_N_GROUPS = 8
SSD_D_STATE = 128
SSD_D_INNER = SSD_N_HEADS * SSD_HEAD_DIM
SSD_GROUP_WIDTH = SSD_D_INNER // SSD_N_GROUPS
SSD_CONV_WIDTH = 4
SSD_CONV_DIM = SSD_D_INNER + 2 * SSD_N_GROUPS * SSD_D_STATE
DT_PAD = 128
SB_HEAD_DIM = 128
SB_EXP2_UNDERFLOW = -160.0
XA_N_HEADS = 4


def _params(*semantics):
    return pltpu.CompilerParams(dimension_semantics=semantics,
                                vmem_limit_bytes=V7X_VMEM_LIMIT_BYTES)


def _dot(a, b):
    return jnp.dot(a, b, preferred_element_type=F32)


def _dot_nt(a, b):
    return lax.dot_general(a, b, (((1,), (1,)), ((), ())), preferred_element_type=F32)


def _sigmoid(x):
    return 0.5 + 0.5 * jnp.tanh(0.5 * x)


def _silu(x):
    half = 0.5 * x
    return half + half * jnp.tanh(half)


def _split3(v):
    p1 = v.astype(BF16)
    r1 = v - p1.astype(F32)
    p2 = r1.astype(BF16)
    p3 = (r1 - p2.astype(F32)).astype(BF16)
    return p1, p2, p3


def _rmsnorm_kernel(x_ref, w_ref, o_ref):
    x = x_ref[...]
    ms = jnp.mean(x * x, axis=-1, keepdims=True)
    o_ref[...] = (x * lax.rsqrt(ms + EPS) * w_ref[...]).astype(o_ref.dtype)


def _rmsnorm(x, w, out_dtype, tm=256):
    t, d = x.shape
    return pl.pallas_call(
        _rmsnorm_kernel,
        grid=(t // tm,),
        in_specs=[pl.BlockSpec((tm, d), lambda i: (i, 0)),
                  pl.BlockSpec((1, d), lambda i: (0, 0))],
        out_specs=pl.BlockSpec((tm, d), lambda i: (i, 0)),
        out_shape=jax.ShapeDtypeStruct((t, d), out_dtype),
        compiler_params=_params("parallel"),
        name="rmsnorm",
    )(x, w.reshape(1, d))


def _mm_kernel(a_ref, b_ref, o_ref):
    o_ref[...] = _dot(a_ref[...], b_ref[...]).astype(o_ref.dtype)


def _gained(a_ref, gain_ref):
    return (a_ref[...].astype(F32) * gain_ref[...]).astype(BF16)


def _mm_rmsnorm_kernel(a_ref, b_ref, ssq_ref, gain_ref, o_ref, *, inv_width):
    row_scale = lax.rsqrt(ssq_ref[:, 0:1] * inv_width + EPS)
    o_ref[...] = (_dot(_gained(a_ref, gain_ref), b_ref[...]) * row_scale).astype(o_ref.dtype)


def _matmul(a, b, out_dtype, tm, tn, name="matmul", b_cols=None, rmsnorm=None):
    m, k = a.shape
    col0, n = b_cols if b_cols is not None else (0, b.shape[1])
    assert col0 % tn == 0 and n % tn == 0 and m % tm == 0
    jb = col0 // tn
    in_specs = [pl.BlockSpec((tm, k), lambda i, j: (i, 0)),
                pl.BlockSpec((k, tn), lambda i, j: (0, j + jb))]
    args = [a, b]
    kern = _mm_kernel
    if rmsnorm is not None:
        row_ssq, gain = rmsnorm
        in_specs += [pl.BlockSpec((tm, LANES), lambda i, j: (i, 0)),
                     pl.BlockSpec((1, k), lambda i, j: (0, 0))]
        args += [row_ssq, gain.astype(F32).reshape(1, k)]
        kern = functools.partial(_mm_rmsnorm_kernel, inv_width=1.0 / k)
    return pl.pallas_call(
        kern,
        grid=(m // tm, n // tn),
        in_specs=in_specs,
        out_specs=pl.BlockSpec((tm, tn), lambda i, j: (i, j)),
        out_shape=jax.ShapeDtypeStruct((m, n), out_dtype),
        compiler_params=_params("parallel", "arbitrary"),
        name=name,
    )(*args)


def _mm_nt_kernel(a_ref, bt_ref, o_ref):
    o_ref[...] = _dot_nt(a_ref[...], bt_ref[...]).astype(o_ref.dtype)


def _matmul_nt(a, bt, out_dtype, tm, tn, rows, name):
    m, k = a.shape
    row0, n = rows
    assert row0 % tn == 0 and n % tn == 0 and m % tm == 0
    jb = row0 // tn
    return pl.pallas_call(
        _mm_nt_kernel,
        grid=(m // tm, n // tn),
        in_specs=[pl.BlockSpec((tm, k), lambda i, j: (i, 0)),
                  pl.BlockSpec((tn, k), lambda i, j: (j + jb, 0))],
        out_specs=pl.BlockSpec((tm, tn), lambda i, j: (i, j)),
        out_shape=jax.ShapeDtypeStruct((m, n), out_dtype),
        compiler_params=_params("parallel", "arbitrary"),
        name=name,
    )(a, bt)


def _mm_res_stats_kernel(a_ref, b_ref, r_ref, o_ref, ob_ref, ssq_ref):
    x = r_ref[...] + _dot(a_ref[...], b_ref[...])
    o_ref[...] = x
    ob_ref[...] = x.astype(ob_ref.dtype)

    @pl.when(pl.program_id(1) == 0)
    def _():
        ssq_ref[...] = jnp.zeros_like(ssq_ref)
    ssq_ref[...] += jnp.broadcast_to(jnp.sum(x * x, axis=-1, keepdims=True), ssq_ref.shape)


def _matmul_res_stats(a, b, residual, tm, tn, name):
    m, k = a.shape
    n = b.shape[1]
    assert n % tn == 0 and m % tm == 0
    return pl.pallas_call(
        _mm_res_stats_kernel,
        grid=(m // tm, n // tn),
        in_specs=[pl.BlockSpec((tm, k), lambda i, j: (i, 0)),
                  pl.BlockSpec((k, tn), lambda i, j: (0, j)),
                  pl.BlockSpec((tm, tn), lambda i, j: (i, j))],
        out_specs=[pl.BlockSpec((tm, tn), lambda i, j: (i, j)),
                   pl.BlockSpec((tm, tn), lambda i, j: (i, j)),
                   pl.BlockSpec((tm, LANES), lambda i, j: (i, 0))],
        out_shape=[jax.ShapeDtypeStruct((m, n), F32),
                   jax.ShapeDtypeStruct((m, n), BF16),
                   jax.ShapeDtypeStruct((m, LANES), F32)],
        compiler_params=_params("parallel", "arbitrary"),
        name=name,
    )(a, b, residual)


def _mm_ksplit_res_kernel(a_ref, b_ref, r_ref, o_ref):
    @pl.when(pl.program_id(2) == 0)
    def _():
        o_ref[...] = r_ref[...]
    o_ref[...] += _dot(a_ref[...], b_ref[...])


def _matmul_ksplit_res(a, b, residual, tm, tn, tk, name="matmul_ksplit"):
    m, k = a.shape
    n = b.shape[1]
    return pl.pallas_call(
        _mm_ksplit_res_kernel,
        grid=(m // tm, n // tn, k // tk),
        in_specs=[pl.BlockSpec((tm, tk), lambda i, j, kk: (i, kk)),
                  pl.BlockSpec((tk, tn), lambda i, j, kk: (kk, j)),
                  pl.BlockSpec((tm, tn), lambda i, j, kk: (i, j))],
        out_specs=pl.BlockSpec((tm, tn), lambda i, j, kk: (i, j)),
        out_shape=jax.ShapeDtypeStruct((m, n), F32),
        compiler_params=_params("parallel", "arbitrary", "arbitrary"),
        name=name,
    )(a, b, residual)


def _merge_kernel(y_ref, o_ref, ws_ref, wb_ref, g1_ref, g2_ref, b1_ref, b2_ref, out_ref):
    branch_ssd = _dot(y_ref[...], ws_ref[...])
    branch_sb = _dot(o_ref[...], wb_ref[...])
    g_ssd = _sigmoid(g1_ref[...] + b1_ref[...])
    g_sb = _sigmoid(g2_ref[...] + b2_ref[...])
    out_ref[...] = (g_ssd * branch_ssd + g_sb * branch_sb).astype(out_ref.dtype)


def _merge(y, o, ws, wb, gate, b_gate, tm=512, tn=512):
    t, d = y.shape
    nj = d // tn
    return pl.pallas_call(
        _merge_kernel,
        grid=(t // tm, nj),
        in_specs=[pl.BlockSpec((tm, d), lambda i, j: (i, 0)),
                  pl.BlockSpec((tm, d), lambda i, j: (i, 0)),
                  pl.BlockSpec((d, tn), lambda i, j: (0, j)),
                  pl.BlockSpec((d, tn), lambda i, j: (0, j)),
                  pl.BlockSpec((tm, tn), lambda i, j: (i, j)),
                  pl.BlockSpec((tm, tn), lambda i, j: (i, j + nj)),
                  pl.BlockSpec((1, tn), lambda i, j: (0, j)),
                  pl.BlockSpec((1, tn), lambda i, j: (0, j + nj))],
        out_specs=pl.BlockSpec((tm, tn), lambda i, j: (i, j)),
        out_shape=jax.ShapeDtypeStruct((t, d), BF16),
        compiler_params=_params("parallel", "arbitrary"),
        name="merge",
    )(y, o, ws, wb, gate, gate, b_gate, b_gate)


def _swiglu_kernel(x_ref, wg_ref, wu_ref, ssq_ref, gain_ref, o_ref, *, inv_width):
    h = _gained(x_ref, gain_ref)
    row_scale = lax.rsqrt(ssq_ref[:, 0:1] * inv_width + EPS)
    g = _dot(h, wg_ref[...]) * row_scale
    u = _dot(h, wu_ref[...]) * row_scale
    o_ref[...] = (_silu(g) * u).astype(o_ref.dtype)


def _swiglu(x, w, row_ssq, gain, tm=2048, tn=256):
    t, d = x.shape
    f = w.shape[1] // 2
    assert f % tn == 0 and t % tm == 0
    nj = f // tn
    return pl.pallas_call(
        functools.partial(_swiglu_kernel, inv_width=1.0 / d),
        grid=(t // tm, nj),
        in_specs=[pl.BlockSpec((tm, d), lambda i, j: (i, 0)),
                  pl.BlockSpec((d, tn), lambda i, j: (0, j)),
                  pl.BlockSpec((d, tn), lambda i, j: (0, j + nj)),
                  pl.BlockSpec((tm, LANES), lambda i, j: (i, 0)),
                  pl.BlockSpec((1, d), lambda i, j: (0, 0))],
        out_specs=pl.BlockSpec((tm, tn), lambda i, j: (i, j)),
        out_shape=jax.ShapeDtypeStruct((t, f), BF16),
        compiler_params=_params("parallel", "arbitrary"),
        name="swiglu",
    )(x, w, w, row_ssq, gain.astype(F32).reshape(1, d))


def _ssd_kernel(xbc_ref, z_ref, dt_ref, cw_ref, cb_ref, dtb_ref, alog_ref, dskip_ref, nw_ref, e_ref,
                h_ref, wg_ref, y_ref, gate_ref, ext_ref, state_ref):
    L = CHUNK
    GW = SSD_GROUP_WIDTH
    N = SSD_D_STATE

    @pl.when(pl.program_id(1) == 0)
    def _():
        state_ref[...] = jnp.zeros_like(state_ref)
        ext_ref[0:8, :] = jnp.zeros((8, SSD_CONV_DIM), F32)

    gate_cols = 256
    groups_per_slice = SSD_N_GROUPS * gate_cols // gate_ref.shape[1]

    def gate_slice(g):
        if g % groups_per_slice == 0:
            s = g // groups_per_slice
            c = slice(s * gate_cols, (s + 1) * gate_cols)
            gate_ref[:, c] = _dot_nt(h_ref[...], wg_ref[c, :])

    ext_ref[8:8 + L, :] = xbc_ref[...]

    def conv_silu(c0, width):
        acc = cb_ref[:, c0:c0 + width]
        for k in range(SSD_CONV_WIDTH):
            r0 = 8 - (SSD_CONV_WIDTH - 1) + k
            acc = acc + cw_ref[k:k + 1, c0:c0 + width] * ext_ref[r0:r0 + L, c0:c0 + width]
        return _silu(acc)

    tl = lax.broadcasted_iota(jnp.int32, (L, L), 0)
    ts = lax.broadcasted_iota(jnp.int32, (L, L), 1)
    tril = (tl >= ts).astype(BF16)
    tril3 = jnp.concatenate([tril, tril, tril], axis=1)

    dt_in = dt_ref[...] + dtb_ref[...]
    dt = jnp.maximum(dt_in, 0.0) + jnp.log1p(jnp.exp(-jnp.abs(dt_in)))
    a_step = dt * (-jnp.exp(alog_ref[...]))
    a_cum_h = _dot(tril3, jnp.concatenate(_split3(a_step), axis=0))
    head_pieces = jnp.concatenate(_split3(dt) + _split3(a_cum_h), axis=0)

    row = lax.broadcasted_iota(jnp.int32, (L, GW), 0)
    col = lax.broadcasted_iota(jnp.int32, (L, GW), 1)
    col_in_head = col & (SSD_HEAD_DIM - 1)
    causal = row >= col_in_head
    diag = row == col_in_head
    head4 = lax.broadcasted_iota(jnp.int32, (L, 4 * SSD_HEAD_DIM), 1) >> 6
    head_masks = [(head4 == r).astype(BF16) for r in range(4)]

    for g in range(SSD_N_GROUPS):
        gate_slice(g)
        c0 = g * GW
        xs = conv_silu(c0, GW)
        bm = conv_silu(SSD_D_INNER + g * N, N)
        cm = conv_silu(SSD_D_INNER + SSD_N_GROUPS * N + g * N, N)

        ex = _dot(head_pieces, e_ref[:, c0:c0 + GW])
        dt_x = ex[0:L] + ex[L:2 * L] + ex[2 * L:3 * L]
        a_cum = ex[3 * L:4 * L] + ex[4 * L:5 * L] + ex[5 * L:6 * L]
        a_last = a_cum[L - 1:L, :]
        exp_a = jnp.exp(a_cum)
        decay_to_end = jnp.exp(a_last - a_cum)

        xd = xs * dt_x

        a_at_s = jnp.sum(jnp.where(diag, a_cum, 0.0), axis=0, keepdims=True)
        lmat = jnp.where(causal, jnp.exp(jnp.minimum(a_cum - a_at_s, 0.0)), 0.0)
        bm_b = bm.astype(BF16)
        cm_b = cm.astype(BF16)
        b_tiled = jnp.concatenate([bm_b] * (GW // L), axis=0)
        cb = _dot_nt(cm_b, b_tiled)
        m_b = (cb * lmat).astype(BF16)

        xd_b = xd.astype(BF16)
        y_parts = []
        for q in range(GW // (4 * SSD_HEAD_DIM)):
            q0 = q * 4 * SSD_HEAD_DIM
            x4 = xd_b[:, q0:q0 + 4 * SSD_HEAD_DIM]
            blockdiag = jnp.concatenate([x4 * mask for mask in head_masks], axis=0)
            y_parts.append(_dot(m_b[:, q0:q0 + 4 * SSD_HEAD_DIM], blockdiag))
        y_diag = jnp.concatenate(y_parts, axis=1)

        prev = state_ref[:, c0:c0 + GW]
        y_off = _dot(cm_b, prev.astype(BF16)) * exp_a
        chunk_state = _dot(bm.T.astype(BF16), (xd * decay_to_end).astype(BF16))
        state_ref[:, c0:c0 + GW] = prev * jnp.exp(a_last) + chunk_state

        y = y_diag + y_off + dskip_ref[:, c0:c0 + GW] * xs

        gated = y * _silu(z_ref[:, c0:c0 + GW])
        ms = jnp.mean(gated * gated, axis=-1, keepdims=True)
        y_ref[:, c0:c0 + GW] = (gated * lax.rsqrt(ms + EPS) * nw_ref[:, c0:c0 + GW]).astype(y_ref.dtype)

    ext_ref[0:8, :] = xbc_ref[L - 8:L, :]


def _ssd(z, xbc, dt_raw, conv_w, conv_b, dt_bias, a_log, d_skip, ssd_norm, h, w_gate, batch, seq, gate_cols,
         gate_tn=512):
    t = batch * seq
    nc = seq // CHUNK
    d = SSD_D_INNER
    k_gate = w_gate.shape[1]
    gate_col0, n_gate = gate_cols
    gate_nj = n_gate // gate_tn
    gate_jb = gate_col0 // gate_tn
    gate_tm = t * gate_nj // (batch * nc)
    assert n_gate % gate_tn == 0 and gate_col0 % gate_tn == 0
    assert (batch * nc) % gate_nj == 0 and t % gate_tm == 0
    head_of_channel = jnp.arange(d, dtype=jnp.int32) // SSD_HEAD_DIM
    expand = (jnp.arange(DT_PAD, dtype=jnp.int32)[:, None] == head_of_channel[None, :]).astype(BF16)
    per_channel = lambda v: jnp.repeat(v.astype(F32), SSD_HEAD_DIM).reshape(1, d)
    per_head = lambda v: jnp.pad(v.astype(F32), (0, DT_PAD - SSD_N_HEADS)).reshape(1, DT_PAD)
    const = lambda shape: pl.BlockSpec(shape, lambda b, c: (0, 0))
    return pl.pallas_call(
        _ssd_kernel,
        grid=(batch, nc),
        in_specs=[pl.BlockSpec((CHUNK, SSD_CONV_DIM), lambda b, c: (b * nc + c, 0)),
                  pl.BlockSpec((CHUNK, d), lambda b, c: (b * nc + c, 0)),
                  pl.BlockSpec((CHUNK, DT_PAD), lambda b, c: (b * nc + c, 0)),
                  const((SSD_CONV_WIDTH, SSD_CONV_DIM)),
                  const((1, SSD_CONV_DIM)),
                  const((1, DT_PAD)),
                  const((1, DT_PAD)),
                  const((1, d)),
                  const((1, d)),
                  const((DT_PAD, d)),
                  pl.BlockSpec((gate_tm, k_gate), lambda b, c: ((b * nc + c) // gate_nj, 0)),
                  pl.BlockSpec((gate_tn, k_gate), lambda b, c: (gate_jb + (b * nc + c) % gate_nj, 0))],
        out_specs=[pl.BlockSpec((CHUNK, d), lambda b, c: (b * nc + c, 0)),
                   pl.BlockSpec((gate_tm, gate_tn), lambda b, c: ((b * nc + c) // gate_nj, (b * nc + c) % gate_nj))],
        out_shape=[jax.ShapeDtypeStruct((t, d), BF16),
                   jax.ShapeDtypeStruct((t, n_gate), F32)],
        scratch_shapes=[pltpu.VMEM((8 + CHUNK, SSD_CONV_DIM), F32),
                        pltpu.VMEM((SSD_D_STATE, d), F32)],
        compiler_params=_params("arbitrary", "arbitrary"),
        name="ssd",
    )(xbc, z, dt_raw, conv_w.astype(F32), conv_b.astype(F32).reshape(1, SSD_CONV_DIM), per_head(dt_bias),
      per_head(a_log), per_channel(d_skip), ssd_norm.astype(F32).reshape(1, d), expand, h, w_gate)


def _sb_kernel(q_ref, k_ref, v_ref, o_ref, *, tb, n_heads_per_step, n_fused, n_together):
    hd = SB_HEAD_DIM
    scale2 = hd ** -0.5 * math.log2(math.e)
    nq = q_ref.shape[0] // tb
    heads = range(n_heads_per_step)
    row = lax.broadcasted_iota(jnp.int32, (tb, tb), 0)
    col = lax.broadcasted_iota(jnp.int32, (tb, tb), 1)
    upper = (row > col).astype(BF16)
    upper2 = jnp.concatenate([upper, upper], axis=0)
    before = col < row

    cols = [slice(h * hd, (h + 1) * hd) for h in heads]

    def key_blocks(qis, blocks, accs, rs):
        q0s = [pl.multiple_of(i * tb, tb) for i in qis]
        k0s = [[pl.multiple_of(j * tb, tb) for j, _ in bl] for bl in blocks]
        units = [(s, t, h) for s in range(len(qis)) for t in range(len(blocks[s])) for h in heads]
        diag = [blocks[s][t][1] for s, t, _ in units]
        zs = [_dot_nt(q_ref[pl.ds(q0s[s], tb), cols[h]], k_ref[pl.ds(k0s[s][t], tb), cols[h]]) * scale2
              for s, t, h in units]
        log_betas = [jnp.minimum(z, 0.0) - jnp.log2(1.0 + jnp.exp2(-jnp.abs(z))) for z in zs]
        log_keeps = [lb - z for lb, z in zip(log_betas, zs)]
        log_keeps = [jnp.where(before, lk, 0.0) if d else lk for lk, d in zip(log_keeps, diag)]
        his = [lk.astype(BF16) for lk in log_keeps]
        los = [(lk - hi.astype(F32)).astype(BF16) for lk, hi in zip(log_keeps, his)]
        suffixes = [_dot(jnp.concatenate([hi, lo], axis=1), upper2) for hi, lo in zip(his, los)]
        partial = [lb + sf for lb, sf in zip(log_betas, suffixes)]
        row_sums = [jnp.sum(lk, axis=-1, keepdims=True) for lk in log_keeps]
        accs = [list(a) for a in accs]
        rs = [list(r) for r in rs]
        for u, (s, t, h) in enumerate(units):
            w = jnp.exp2(partial[u] + rs[s][h])
            if diag[u]:
                w = jnp.where(before, w, 0.0)
            accs[s][h] = accs[s][h] + _dot(w.astype(BF16), v_ref[pl.ds(k0s[s][t], tb), cols[h]])
            rs[s][h] = rs[s][h] + row_sums[u]
        return tuple(tuple(a) for a in accs), tuple(tuple(r) for r in rs)

    def still_live(rs):
        worst = functools.reduce(jnp.maximum, rs)
        return (jnp.max(worst) > SB_EXP2_UNDERFLOW).astype(jnp.int32)

    def q_blocks(qis, n_first):
        qis = [jnp.asarray(i, jnp.int32) for i in qis]
        accs = tuple(tuple(jnp.zeros((tb, hd), F32) for _ in heads) for _ in qis)
        rs = tuple(tuple(jnp.zeros((tb, 1), F32) for _ in heads) for _ in qis)
        accs, rs = key_blocks(qis, [[(i - t, t == 0) for t in range(n_first)] for i in qis], accs, rs)

        def cond(c):
            return jnp.logical_and(c[0] >= 0, c[1] > 0)

        for s, i in enumerate(qis):
            def body(c, i=i):
                a, r = key_blocks([i], [[(c[0], False)]], (c[2],), (c[3],))
                return c[0] - 1, still_live(r[0]), a[0], r[0]

            _, _, acc_s, _ = lax.while_loop(cond, body, (i - n_first, still_live(rs[s]), accs[s], rs[s]))
            q0 = pl.multiple_of(i * tb, tb)
            for h in heads:
                o_ref[pl.ds(q0, tb), cols[h]] = acc_s[h].astype(o_ref.dtype)

    first_steady = -(-(n_fused - 1) // n_together) * n_together
    assert (nq - first_steady) % n_together == 0
    for i in range(min(first_steady, nq)):
        q_blocks([i], min(i + 1, n_fused))

    def steady(p, carry):
        i0 = first_steady + p * n_together
        q_blocks([i0 + s for s in range(n_together)], n_fused)
        return carry

    lax.fori_loop(0, (nq - first_steady) // n_together, steady, 0)


def _sb_attention(qkv, batch, seq, n_heads, tb=128, n_heads_per_step=4, n_fused=3, n_together=2):
    t = batch * seq
    hw = n_heads_per_step * SB_HEAD_DIM
    ng = n_heads // n_heads_per_step
    return pl.pallas_call(
        functools.partial(_sb_kernel, tb=tb, n_heads_per_step=n_heads_per_step, n_fused=n_fused,
                          n_together=n_together),
        grid=(batch, ng),
        in_specs=[pl.BlockSpec((seq, hw), lambda b, h: (b, h)),
                  pl.BlockSpec((seq, hw), lambda b, h: (b, ng + h)),
                  pl.BlockSpec((seq, hw), lambda b, h: (b, 2 * ng + h))],
        out_specs=pl.BlockSpec((seq, hw), lambda b, h: (b, h)),
        out_shape=jax.ShapeDtypeStruct((t, n_heads * SB_HEAD_DIM), BF16),
        compiler_params=_params("parallel", "parallel"),
        name="sb_attention",
    )(qkv, qkv, qkv)


def _xattn_kernel(q_ref, kv_ref, o_ref, *, n_heads, head_dim):
    scale = head_dim ** -0.5
    d = n_heads * head_dim
    for h in range(n_heads):
        c0 = h * head_dim
        s = _dot_nt(q_ref[:, c0:c0 + head_dim], kv_ref[:, c0:c0 + head_dim]) * scale
        p = jnp.exp(s - jnp.max(s, axis=-1, keepdims=True))
        p = p / jnp.sum(p, axis=-1, keepdims=True)
        o_ref[:, c0:c0 + head_dim] = _dot(p.astype(BF16), kv_ref[:, d + c0:d + c0 + head_dim]).astype(o_ref.dtype)


def _xattn(q, kv, batch, seq, n_mem, n_heads, tq=512):
    t, d = q.shape
    nq = seq // tq
    return pl.pallas_call(
        functools.partial(_xattn_kernel, n_heads=n_heads, head_dim=d // n_heads),
        grid=(batch, nq),
        in_specs=[pl.BlockSpec((tq, d), lambda b, i: (b * nq + i, 0)),
                  pl.BlockSpec((n_mem, 2 * d), lambda b, i: (b, 0))],
        out_specs=pl.BlockSpec((tq, d), lambda b, i: (b * nq + i, 0)),
        out_shape=jax.ShapeDtypeStruct((t, d), BF16),
        compiler_params=_params("parallel", "arbitrary"),
        name="xattn",
    )(q, kv)


def _layer(x, mem, norm_mix, w_in, b_gate, conv_w, conv_b, dt_bias, a_log, d_skip, ssd_norm,
           w_ssd_out, w_sb_out, w_out, norm_xa, norm_mem, w_xa_q, w_xa_kv, w_xa_o,
           norm_ffn, w_ffn_in, w_ffn_out, batch, seq):
    d = x.shape[1]
    n_mem = mem.shape[0] // batch
    d_sb = w_sb_out.shape[0]
    n_sb_heads = d_sb // SB_HEAD_DIM
    d_ff = w_ffn_out.shape[0]

    o_dt = SSD_D_INNER + SSD_CONV_DIM
    o_q = o_dt + SSD_N_HEADS
    o_gate = o_q + 3 * d_sb
    w_in_t = jnp.transpose(w_in).astype(BF16)
    w_qkvg_t = w_in_t[o_q:]

    h = _rmsnorm(x, norm_mix, BF16)
    z = _matmul_nt(h, w_in_t, F32, 1024, 1024, (0, SSD_D_INNER), name="proj_z")
    xbc = _matmul_nt(h, w_in_t, F32, 1024, 1024, (SSD_D_INNER, SSD_CONV_DIM), name="proj_xbc")
    dt_raw = _matmul_nt(h, w_in_t, F32, 1024, DT_PAD, (o_dt, DT_PAD), name="proj_dt")
    qkv = _matmul_nt(h, w_qkvg_t, BF16, 1024, 1024, (0, 3 * d_sb), name="proj_qkv")

    y_ssd, gate = _ssd(z, xbc, dt_raw, conv_w, conv_b, dt_bias, a_log, d_skip, ssd_norm, h, w_qkvg_t, batch, seq,
                       gate_cols=(3 * d_sb, 2 * d))
    o_sb = _sb_attention(qkv, batch, seq, n_sb_heads)

    merged = _merge(y_ssd, o_sb, w_ssd_out.astype(BF16), w_sb_out.astype(BF16), gate,
                    b_gate.astype(F32).reshape(1, 2 * d))
    x, x_b, x_ssq = _matmul_res_stats(merged, w_out.astype(BF16), x, 1024, 512, name="out_proj")

    hm = _rmsnorm(mem, norm_mem, BF16)
    q = _matmul(x_b, w_xa_q.astype(BF16), BF16, 1024, 1024, name="xa_q", rmsnorm=(x_ssq, norm_xa))
    kv = _matmul(hm, w_xa_kv.astype(BF16), BF16, mem.shape[0], 1024, name="xa_kv")
    o = _xattn(q, kv, batch, seq, n_mem, XA_N_HEADS)
    x, x_b, x_ssq = _matmul_res_stats(o, w_xa_o.astype(BF16), x, 1024, 512, name="xa_o")

    act = _swiglu(x_b, w_ffn_in.astype(BF16), x_ssq, norm_ffn)
    x = _matmul_ksplit_res(act, w_ffn_out.astype(BF16), x, 512, 512, d_ff, name="ffn_out")
    return x


def kernel(x, mem, norm_mix, w_in, b_gate, conv_w, conv_b, dt_bias, a_log, d_skip, ssd_norm, w_ssd_out, w_sb_out, w_out, norm_xa, norm_mem, w_xa_q, w_xa_kv, w_xa_o, norm_ffn, w_ffn_in, w_ffn_out, norm_final):
    batch, seq, d = x.shape
    xf = x.reshape(batch * seq, d)
    memf = mem.reshape(-1, d)
    for l in range(norm_mix.shape[0]):
        xf = _layer(xf, memf, norm_mix[l], w_in[l], b_gate[l], conv_w[l], conv_b[l], dt_bias[l], a_log[l],
                    d_skip[l], ssd_norm[l], w_ssd_out[l], w_sb_out[l], w_out[l], norm_xa[l], norm_mem[l],
                    w_xa_q[l], w_xa_kv[l], w_xa_o[l], norm_ffn[l], w_ffn_in[l], w_ffn_out[l], batch, seq)
    out = _rmsnorm(xf, norm_final, x.dtype)
    return out.reshape(batch, seq, d)
```

```python
import functools
import math

import jax
import jax.numpy as jnp
from jax import lax
from jax.experimental import pallas as pl
from jax.experimental.pallas import tpu as pltpu

F32 = jnp.float32
BF16 = jnp.bfloat16
EPS = 1e-6

V7X_VMEM_LIMIT_BYTES = 56 * 1024 * 1024
LANES = 128

CHUNK = 64
SSD_HEAD_DIM = 64
SSD_N_HEADS = 64
SSD_N_GROUPS = 8
SSD_D_STATE = 128
SSD_D_INNER = SSD_N_HEADS * SSD_HEAD_DIM
SSD_GROUP_WIDTH = SSD_D_INNER // SSD_N_GROUPS
SSD_CONV_WIDTH = 4
SSD_CONV_DIM = SSD_D_INNER + 2 * SSD_N_GROUPS * SSD_D_STATE
DT_PAD = 128
SB_HEAD_DIM = 128
SB_EXP2_UNDERFLOW = -160.0
XA_N_HEADS = 4


def _params(*semantics):
    return pltpu.CompilerParams(dimension_semantics=semantics,
                                vmem_limit_bytes=V7X_VMEM_LIMIT_BYTES)


def _dot(a, b):
    return jnp.dot(a, b, preferred_element_type=F32)


def _dot_nt(a, b):
    return lax.dot_general(a, b, (((1,), (1,)), ((), ())), preferred_element_type=F32)


def _sigmoid(x):
    return 0.5 + 0.5 * jnp.tanh(0.5 * x)


def _silu(x):
    half = 0.5 * x
    return half + half * jnp.tanh(half)


def _split3(v):
    p1 = v.astype(BF16)
    r1 = v - p1.astype(F32)
    p2 = r1.astype(BF16)
    p3 = (r1 - p2.astype(F32)).astype(BF16)
    return p1, p2, p3


def _rmsnorm_kernel(x_ref, w_ref, o_ref):
    x = x_ref[...]
    ms = jnp.mean(x * x, axis=-1, keepdims=True)
    o_ref[...] = (x * lax.rsqrt(ms + EPS) * w_ref[...]).astype(o_ref.dtype)


def _rmsnorm(x, w, out_dtype, tm=256):
    t, d = x.shape
    return pl.pallas_call(
        _rmsnorm_kernel,
        grid=(t // tm,),
        in_specs=[pl.BlockSpec((tm, d), lambda i: (i, 0)),
                  pl.BlockSpec((1, d), lambda i: (0, 0))],
        out_specs=pl.BlockSpec((tm, d), lambda i: (i, 0)),
        out_shape=jax.ShapeDtypeStruct((t, d), out_dtype),
        compiler_params=_params("parallel"),
        name="rmsnorm",
    )(x, w.reshape(1, d))


def _mm_kernel(a_ref, b_ref, o_ref):
    o_ref[...] = _dot(a_ref[...], b_ref[...]).astype(o_ref.dtype)


def _gained(a_ref, gain_ref):
    return (a_ref[...].astype(F32) * gain_ref[...]).astype(BF16)


def _mm_rmsnorm_kernel(a_ref, b_ref, ssq_ref, gain_ref, o_ref, *, inv_width):
    row_scale = lax.rsqrt(ssq_ref[:, 0:1] * inv_width + EPS)
    o_ref[...] = (_dot(_gained(a_ref, gain_ref), b_ref[...]) * row_scale).astype(o_ref.dtype)


def _matmul(a, b, out_dtype, tm, tn, name="matmul", b_cols=None, rmsnorm=None):
    m, k = a.shape
    col0, n = b_cols if b_cols is not None else (0, b.shape[1])
    assert col0 % tn == 0 and n % tn == 0 and m % tm == 0
    jb = col0 // tn
    in_specs = [pl.BlockSpec((tm, k), lambda i, j: (i, 0)),
                pl.BlockSpec((k, tn), lambda i, j: (0, j + jb))]
    args = [a, b]
    kern = _mm_kernel
    if rmsnorm is not None:
        row_ssq, gain = rmsnorm
        in_specs += [pl.BlockSpec((tm, LANES), lambda i, j: (i, 0)),
                     pl.BlockSpec((1, k), lambda i, j: (0, 0))]
        args += [row_ssq, gain.astype(F32).reshape(1, k)]
        kern = functools.partial(_mm_rmsnorm_kernel, inv_width=1.0 / k)
    return pl.pallas_call(
        kern,
        grid=(m // tm, n // tn),
        in_specs=in_specs,
        out_specs=pl.BlockSpec((tm, tn), lambda i, j: (i, j)),
        out_shape=jax.ShapeDtypeStruct((m, n), out_dtype),
        compiler_params=_params("parallel", "arbitrary"),
        name=name,
    )(*args)


def _mm_nt_kernel(a_ref, bt_ref, o_ref):
    o_ref[...] = _dot_nt(a_ref[...], bt_ref[...]).astype(o_ref.dtype)


def _matmul_nt(a, bt, out_dtype, tm, tn, rows, name):
    m, k = a.shape
    row0, n = rows
    assert row0 % tn == 0 and n % tn == 0 and m % tm == 0
    jb = row0 // tn
    return pl.pallas_call(
        _mm_nt_kernel,
        grid=(m // tm, n // tn),
        in_specs=[pl.BlockSpec((tm, k), lambda i, j: (i, 0)),
                  pl.BlockSpec((tn, k), lambda i, j: (j + jb, 0))],
        out_specs=pl.BlockSpec((tm, tn), lambda i, j: (i, j)),
        out_shape=jax.ShapeDtypeStruct((m, n), out_dtype),
        compiler_params=_params("parallel", "arbitrary"),
        name=name,
    )(a, bt)


def _mm_res_stats_kernel(a_ref, b_ref, r_ref, o_ref, ob_ref, ssq_ref):
    x = r_ref[...] + _dot(a_ref[...], b_ref[...])
    o_ref[...] = x
    ob_ref[...] = x.astype(ob_ref.dtype)

    @pl.when(pl.program_id(1) == 0)
    def _():
        ssq_ref[...] = jnp.zeros_like(ssq_ref)
    ssq_ref[...] += jnp.broadcast_to(jnp.sum(x * x, axis=-1, keepdims=True), ssq_ref.shape)


def _matmul_res_stats(a, b, residual, tm, tn, name):
    m, k = a.shape
    n = b.shape[1]
    assert n % tn == 0 and m % tm == 0
    return pl.pallas_call(
        _mm_res_stats_kernel,
        grid=(m // tm, n // tn),
        in_specs=[pl.BlockSpec((tm, k), lambda i, j: (i, 0)),
                  pl.BlockSpec((k, tn), lambda i, j: (0, j)),
                  pl.BlockSpec((tm, tn), lambda i, j: (i, j))],
        out_specs=[pl.BlockSpec((tm, tn), lambda i, j: (i, j)),
                   pl.BlockSpec((tm, tn), lambda i, j: (i, j)),
                   pl.BlockSpec((tm, LANES), lambda i, j: (i, 0))],
        out_shape=[jax.ShapeDtypeStruct((m, n), F32),
                   jax.ShapeDtypeStruct((m, n), BF16),
                   jax.ShapeDtypeStruct((m, LANES), F32)],
        compiler_params=_params("parallel", "arbitrary"),
        name=name,
    )(a, b, residual)


def _mm_ksplit_res_kernel(a_ref, b_ref, r_ref, o_ref):
    @pl.when(pl.program_id(2) == 0)
    def _():
        o_ref[...] = r_ref[...]
    o_ref[...] += _dot(a_ref[...], b_ref[...])


def _matmul_ksplit_res(a, b, residual, tm, tn, tk, name="matmul_ksplit"):
    m, k = a.shape
    n = b.shape[1]
    return pl.pallas_call(
        _mm_ksplit_res_kernel,
        grid=(m // tm, n // tn, k // tk),
        in_specs=[pl.BlockSpec((tm, tk), lambda i, j, kk: (i, kk)),
                  pl.BlockSpec((tk, tn), lambda i, j, kk: (kk, j)),
                  pl.BlockSpec((tm, tn), lambda i, j, kk: (i, j))],
        out_specs=pl.BlockSpec((tm, tn), lambda i, j, kk: (i, j)),
        out_shape=jax.ShapeDtypeStruct((m, n), F32),
        compiler_params=_params("parallel", "arbitrary", "arbitrary"),
        name=name,
    )(a, b, residual)


def _merge_kernel(y_ref, o_ref, ws_ref, wb_ref, g1_ref, g2_ref, b1_ref, b2_ref, out_ref):
    branch_ssd = _dot(y_ref[...], ws_ref[...])
    branch_sb = _dot(o_ref[...], wb_ref[...])
    g_ssd = _sigmoid(g1_ref[...] + b1_ref[...])
    g_sb = _sigmoid(g2_ref[...] + b2_ref[...])
    out_ref[...] = (g_ssd * branch_ssd + g_sb * branch_sb).astype(out_ref.dtype)


def _merge(y, o, ws, wb, gate, b_gate, tm=512, tn=512):
    t, d = y.shape
    nj = d // tn
    return pl.pallas_call(
        _merge_kernel,
        grid=(t // tm, nj),
        in_specs=[pl.BlockSpec((tm, d), lambda i, j: (i, 0)),
                  pl.BlockSpec((tm, d), lambda i, j: (i, 0)),
                  pl.BlockSpec((d, tn), lambda i, j: (0, j)),
                  pl.BlockSpec((d, tn), lambda i, j: (0, j)),
                  pl.BlockSpec((tm, tn), lambda i, j: (i, j)),
                  pl.BlockSpec((tm, tn), lambda i, j: (i, j + nj)),
                  pl.BlockSpec((1, tn), lambda i, j: (0, j)),
                  pl.BlockSpec((1, tn), lambda i, j: (0, j + nj))],
        out_specs=pl.BlockSpec((tm, tn), lambda i, j: (i, j)),
        out_shape=jax.ShapeDtypeStruct((t, d), BF16),
        compiler_params=_params("parallel", "arbitrary"),
        name="merge",
    )(y, o, ws, wb, gate, gate, b_gate, b_gate)


def _swiglu_kernel(x_ref, wg_ref, wu_ref, ssq_ref, gain_ref, o_ref, *, inv_width):
    h = _gained(x_ref, gain_ref)
    row_scale = lax.rsqrt(ssq_ref[:, 0:1] * inv_width + EPS)
    g = _dot(h, wg_ref[...]) * row_scale
    u = _dot(h, wu_ref[...]) * row_scale
    o_ref[...] = (_silu(g) * u).astype(o_ref.dtype)


def _swiglu(x, w, row_ssq, gain, tm=2048, tn=256):
    t, d = x.shape
    f = w.shape[1] // 2
    assert f % tn == 0 and t % tm == 0
    nj = f // tn
    return pl.pallas_call(
        functools.partial(_swiglu_kernel, inv_width=1.0 / d),
        grid=(t // tm, nj),
        in_specs=[pl.BlockSpec((tm, d), lambda i, j: (i, 0)),
                  pl.BlockSpec((d, tn), lambda i, j: (0, j)),
                  pl.BlockSpec((d, tn), lambda i, j: (0, j + nj)),
                  pl.BlockSpec((tm, LANES), lambda i, j: (i, 0)),
                  pl.BlockSpec((1, d), lambda i, j: (0, 0))],
        out_specs=pl.BlockSpec((tm, tn), lambda i, j: (i, j)),
        out_shape=jax.ShapeDtypeStruct((t, f), BF16),
        compiler_params=_params("parallel", "arbitrary"),
        name="swiglu",
    )(x, w, w, row_ssq, gain.astype(F32).reshape(1, d))


def _ssd_kernel(xbc_ref, z_ref, dt_ref, cw_ref, cb_ref, dtb_ref, alog_ref, dskip_ref, nw_ref, e_ref,
                h_ref, wg_ref, y_ref, gate_ref, ext_ref, state_ref):
    L = CHUNK
    GW = SSD_GROUP_WIDTH
    N = SSD_D_STATE

    @pl.when(pl.program_id(1) == 0)
    def _():
        state_ref[...] = jnp.zeros_like(state_ref)
        ext_ref[0:8, :] = jnp.zeros((8, SSD_CONV_DIM), F32)

    gate_cols = 256
    groups_per_slice = SSD_N_GROUPS * gate_cols // gate_ref.shape[1]

    def gate_slice(g):
        if g % groups_per_slice == 0:
            s = g // groups_per_slice
            c = slice(s * gate_cols, (s + 1) * gate_cols)
            gate_ref[:, c] = _dot_nt(h_ref[...], wg_ref[c, :])

    ext_ref[8:8 + L, :] = xbc_ref[...]

    def conv_silu(c0, width):
        acc = cb_ref[:, c0:c0 + width]
        for k in range(SSD_CONV_WIDTH):
            r0 = 8 - (SSD_CONV_WIDTH - 1) + k
            acc = acc + cw_ref[k:k + 1, c0:c0 + width] * ext_ref[r0:r0 + L, c0:c0 + width]
        return _silu(acc)

    tl = lax.broadcasted_iota(jnp.int32, (L, L), 0)
    ts = lax.broadcasted_iota(jnp.int32, (L, L), 1)
    tril = (tl >= ts).astype(BF16)
    tril3 = jnp.concatenate([tril, tril, tril], axis=1)

    dt_in = dt_ref[...] + dtb_ref[...]
    dt = jnp.maximum(dt_in, 0.0) + jnp.log1p(jnp.exp(-jnp.abs(dt_in)))
    a_step = dt * (-jnp.exp(alog_ref[...]))
    a_cum_h = _dot(tril3, jnp.concatenate(_split3(a_step), axis=0))
    head_pieces = jnp.concatenate(_split3(dt) + _split3(a_cum_h), axis=0)

    row = lax.broadcasted_iota(jnp.int32, (L, GW), 0)
    col = lax.broadcasted_iota(jnp.int32, (L, GW), 1)
    col_in_head = col & (SSD_HEAD_DIM - 1)
    causal = row >= col_in_head
    diag = row == col_in_head
    head4 = lax.broadcasted_iota(jnp.int32, (L, 4 * SSD_HEAD_DIM), 1) >> 6
    head_masks = [(head4 == r).astype(BF16) for r in range(4)]

    for g in range(SSD_N_GROUPS):
        gate_slice(g)
        c0 = g * GW
        xs = conv_silu(c0, GW)
        bm = conv_silu(SSD_D_INNER + g * N, N)
        cm = conv_silu(SSD_D_INNER + SSD_N_GROUPS * N + g * N, N)

        ex = _dot(head_pieces, e_ref[:, c0:c0 + GW])
        dt_x = ex[0:L] + ex[L:2 * L] + ex[2 * L:3 * L]
        a_cum = ex[3 * L:4 * L] + ex[4 * L:5 * L] + ex[5 * L:6 * L]
        a_last = a_cum[L - 1:L, :]
        exp_a = jnp.exp(a_cum)
        decay_to_end = jnp.exp(a_last - a_cum)

        xd = xs * dt_x

        a_at_s = jnp.sum(jnp.where(diag, a_cum, 0.0), axis=0, keepdims=True)
        lmat = jnp.where(causal, jnp.exp(jnp.minimum(a_cum - a_at_s, 0.0)), 0.0)
        bm_b = bm.astype(BF16)
        cm_b = cm.astype(BF16)
        b_tiled = jnp.concatenate([bm_b] * (GW // L), axis=0)
        cb = _dot_nt(cm_b, b_tiled)
        m_b = (cb * lmat).astype(BF16)

        xd_b = xd.astype(BF16)
        y_parts = []
        for q in range(GW // (4 * SSD_HEAD_DIM)):
            q0 = q * 4 * SSD_HEAD_DIM
            x4 = xd_b[:, q0:q0 + 4 * SSD_HEAD_DIM]
            blockdiag = jnp.concatenate([x4 * mask for mask in head_masks], axis=0)
            y_parts.append(_dot(m_b[:, q0:q0 + 4 * SSD_HEAD_DIM], blockdiag))
        y_diag = jnp.concatenate(y_parts, axis=1)

        prev = state_ref[:, c0:c0 + GW]
        y_off = _dot(cm_b, prev.astype(BF16)) * exp_a
        chunk_state = _dot(bm.T.astype(BF16), (xd * decay_to_end).astype(BF16))
        state_ref[:, c0:c0 + GW] = prev * jnp.exp(a_last) + chunk_state

        y = y_diag + y_off + dskip_ref[:, c0:c0 + GW] * xs

        gated = y * _silu(z_ref[:, c0:c0 + GW])
        ms = jnp.mean(gated * gated, axis=-1, keepdims=True)
        y_ref[:, c0:c0 + GW] = (gated * lax.rsqrt(ms + EPS) * nw_ref[:, c0:c0 + GW]).astype(y_ref.dtype)

    ext_ref[0:8, :] = xbc_ref[L - 8:L, :]


def _ssd(z, xbc, dt_raw, conv_w, conv_b, dt_bias, a_log, d_skip, ssd_norm, h, w_gate, batch, seq, gate_cols,
         gate_tn=512):
    t = batch * seq
    nc = seq // CHUNK
    d = SSD_D_INNER
    k_gate = w_gate.shape[1]
    gate_col0, n_gate = gate_cols
    gate_nj = n_gate // gate_tn
    gate_jb = gate_col0 // gate_tn
    gate_tm = t * gate_nj // (batch * nc)
    assert n_gate % gate_tn == 0 and gate_col0 % gate_tn == 0
    assert (batch * nc) % gate_nj == 0 and t % gate_tm == 0
    head_of_channel = jnp.arange(d, dtype=jnp.int32) // SSD_HEAD_DIM
    expand = (jnp.arange(DT_PAD, dtype=jnp.int32)[:, None] == head_of_channel[None, :]).astype(BF16)
    per_channel = lambda v: jnp.repeat(v.astype(F32), SSD_HEAD_DIM).reshape(1, d)
    per_head = lambda v: jnp.pad(v.astype(F32), (0, DT_PAD - SSD_N_HEADS)).reshape(1, DT_PAD)
    const = lambda shape: pl.BlockSpec(shape, lambda b, c: (0, 0))
    return pl.pallas_call(
        _ssd_kernel,
        grid=(batch, nc),
        in_specs=[pl.BlockSpec((CHUNK, SSD_CONV_DIM), lambda b, c: (b * nc + c, 0)),
                  pl.BlockSpec((CHUNK, d), lambda b, c: (b * nc + c, 0)),
                  pl.BlockSpec((CHUNK, DT_PAD), lambda b, c: (b * nc + c, 0)),
                  const((SSD_CONV_WIDTH, SSD_CONV_DIM)),
                  const((1, SSD_CONV_DIM)),
                  const((1, DT_PAD)),
                  const((1, DT_PAD)),
                  const((1, d)),
                  const((1, d)),
                  const((DT_PAD, d)),
                  pl.BlockSpec((gate_tm, k_gate), lambda b, c: ((b * nc + c) // gate_nj, 0)),
                  pl.BlockSpec((gate_tn, k_gate), lambda b, c: (gate_jb + (b * nc + c) % gate_nj, 0))],
        out_specs=[pl.BlockSpec((CHUNK, d), lambda b, c: (b * nc + c, 0)),
                   pl.BlockSpec((gate_tm, gate_tn), lambda b, c: ((b * nc + c) // gate_nj, (b * nc + c) % gate_nj))],
        out_shape=[jax.ShapeDtypeStruct((t, d), BF16),
                   jax.ShapeDtypeStruct((t, n_gate), F32)],
        scratch_shapes=[pltpu.VMEM((8 + CHUNK, SSD_CONV_DIM), F32),
                        pltpu.VMEM((SSD_D_STATE, d), F32)],
        compiler_params=_params("arbitrary", "arbitrary"),
        name="ssd",
    )(xbc, z, dt_raw, conv_w.astype(F32), conv_b.astype(F32).reshape(1, SSD_CONV_DIM), per_head(dt_bias),
      per_head(a_log), per_channel(d_skip), ssd_norm.astype(F32).reshape(1, d), expand, h, w_gate)


def _sb_kernel(q_ref, k_ref, v_ref, o_ref, *, tb, n_heads_per_step, n_fused, n_together):
    hd = SB_HEAD_DIM
    scale2 = hd ** -0.5 * math.log2(math.e)
    nq = q_ref.shape[0] // tb
    heads = range(n_heads_per_step)
    row = lax.broadcasted_iota(jnp.int32, (tb, tb), 0)
    col = lax.broadcasted_iota(jnp.int32, (tb, tb), 1)
    upper = (row > col).astype(BF16)
    upper2 = jnp.concatenate([upper, upper], axis=0)
    before = col < row

    cols = [slice(h * hd, (h + 1) * hd) for h in heads]

    def key_blocks(qis, blocks, accs, rs):
        q0s = [pl.multiple_of(i * tb, tb) for i in qis]
        k0s = [[pl.multiple_of(j * tb, tb) for j, _ in bl] for bl in blocks]
        units = [(s, t, h) for s in range(len(qis)) for t in range(len(blocks[s])) for h in heads]
        diag = [blocks[s][t][1] for s, t, _ in units]
        zs = [_dot_nt(q_ref[pl.ds(q0s[s], tb), cols[h]], k_ref[pl.ds(k0s[s][t], tb), cols[h]]) * scale2
              for s, t, h in units]
        log_betas = [jnp.minimum(z, 0.0) - jnp.log2(1.0 + jnp.exp2(-jnp.abs(z))) for z in zs]
        log_keeps = [lb - z for lb, z in zip(log_betas, zs)]
        log_keeps = [jnp.where(before, lk, 0.0) if d else lk for lk, d in zip(log_keeps, diag)]
        his = [lk.astype(BF16) for lk in log_keeps]
        los = [(lk - hi.astype(F32)).astype(BF16) for lk, hi in zip(log_keeps, his)]
        suffixes = [_dot(jnp.concatenate([hi, lo], axis=1), upper2) for hi, lo in zip(his, los)]
        partial = [lb + sf for lb, sf in zip(log_betas, suffixes)]
        row_sums = [jnp.sum(lk, axis=-1, keepdims=True) for lk in log_keeps]
        accs = [list(a) for a in accs]
        rs = [list(r) for r in rs]
        for u, (s, t, h) in enumerate(units):
            w = jnp.exp2(partial[u] + rs[s][h])
            if diag[u]:
                w = jnp.where(before, w, 0.0)
            accs[s][h] = accs[s][h] + _dot(w.astype(BF16), v_ref[pl.ds(k0s[s][t], tb), cols[h]])
            rs[s][h] = rs[s][h] + row_sums[u]
        return tuple(tuple(a) for a in accs), tuple(tuple(r) for r in rs)

    def still_live(rs):
        worst = functools.reduce(jnp.maximum, rs)
        return (jnp.max(worst) > SB_EXP2_UNDERFLOW).astype(jnp.int32)

    def q_blocks(qis, n_first):
        qis = [jnp.asarray(i, jnp.int32) for i in qis]
        accs = tuple(tuple(jnp.zeros((tb, hd), F32) for _ in heads) for _ in qis)
        rs = tuple(tuple(jnp.zeros((tb, 1), F32) for _ in heads) for _ in qis)
        accs, rs = key_blocks(qis, [[(i - t, t == 0) for t in range(n_first)] for i in qis], accs, rs)

        def cond(c):
            return jnp.logical_and(c[0] >= 0, c[1] > 0)

        for s, i in enumerate(qis):
            def body(c, i=i):
                a, r = key_blocks([i], [[(c[0], False)]], (c[2],), (c[3],))
                return c[0] - 1, still_live(r[0]), a[0], r[0]

            _, _, acc_s, _ = lax.while_loop(cond, body, (i - n_first, still_live(rs[s]), accs[s], rs[s]))
            q0 = pl.multiple_of(i * tb, tb)
            for h in heads:
                o_ref[pl.ds(q0, tb), cols[h]] = acc_s[h].astype(o_ref.dtype)

    first_steady = -(-(n_fused - 1) // n_together) * n_together
    assert (nq - first_steady) % n_together == 0
    for i in range(min(first_steady, nq)):
        q_blocks([i], min(i + 1, n_fused))

    def steady(p, carry):
        i0 = first_steady + p * n_together
        q_blocks([i0 + s for s in range(n_together)], n_fused)
        return carry

    lax.fori_loop(0, (nq - first_steady) // n_together, steady, 0)


def _sb_attention(qkv, batch, seq, n_heads, tb=128, n_heads_per_step=4, n_fused=3, n_together=4):
    t = batch * seq
    hw = n_heads_per_step * SB_HEAD_DIM
    ng = n_heads // n_heads_per_step
    return pl.pallas_call(
        functools.partial(_sb_kernel, tb=tb, n_heads_per_step=n_heads_per_step, n_fused=n_fused,
                          n_together=n_together),
        grid=(batch, ng),
        in_specs=[pl.BlockSpec((seq, hw), lambda b, h: (b, h)),
                  pl.BlockSpec((seq, hw), lambda b, h: (b, ng + h)),
                  pl.BlockSpec((seq, hw), lambda b, h: (b, 2 * ng + h))],
        out_specs=pl.BlockSpec((seq, hw), lambda b, h: (b, h)),
        out_shape=jax.ShapeDtypeStruct((t, n_heads * SB_HEAD_DIM), BF16),
        compiler_params=_params("parallel", "parallel"),
        name="sb_attention",
    )(qkv, qkv, qkv)


def _xattn_kernel(q_ref, kv_ref, o_ref, *, n_heads, head_dim):
    scale = head_dim ** -0.5
    d = n_heads * head_dim
    for h in range(n_heads):
        c0 = h * head_dim
        s = _dot_nt(q_ref[:, c0:c0 + head_dim], kv_ref[:, c0:c0 + head_dim]) * scale
        p = jnp.exp(s - jnp.max(s, axis=-1, keepdims=True))
        p = p / jnp.sum(p, axis=-1, keepdims=True)
        o_ref[:, c0:c0 + head_dim] = _dot(p.astype(BF16), kv_ref[:, d + c0:d + c0 + head_dim]).astype(o_ref.dtype)


def _xattn(q, kv, batch, seq, n_mem, n_heads, tq=512):
    t, d = q.shape
    nq = seq // tq
    return pl.pallas_call(
        functools.partial(_xattn_kernel, n_heads=n_heads, head_dim=d // n_heads),
        grid=(batch, nq),
        in_specs=[pl.BlockSpec((tq, d), lambda b, i: (b * nq + i, 0)),
                  pl.BlockSpec((n_mem, 2 * d), lambda b, i: (b, 0))],
        out_specs=pl.BlockSpec((tq, d), lambda b, i: (b * nq + i, 0)),
        out_shape=jax.ShapeDtypeStruct((t, d), BF16),
        compiler_params=_params("parallel", "arbitrary"),
        name="xattn",
    )(q, kv)


def _layer(x, mem, norm_mix, w_in, b_gate, conv_w, conv_b, dt_bias, a_log, d_skip, ssd_norm,
           w_ssd_out, w_sb_out, w_out, norm_xa, norm_mem, w_xa_q, w_xa_kv, w_xa_o,
           norm_ffn, w_ffn_in, w_ffn_out, batch, seq):
    d = x.shape[1]
    n_mem = mem.shape[0] // batch
    d_sb = w_sb_out.shape[0]
    n_sb_heads = d_sb // SB_HEAD_DIM
    d_ff = w_ffn_out.shape[0]

    o_dt = SSD_D_INNER + SSD_CONV_DIM
    o_q = o_dt + SSD_N_HEADS
    o_gate = o_q + 3 * d_sb
    w_in_t = jnp.transpose(w_in).astype(BF16)
    w_qkvg_t = w_in_t[o_q:]

    h = _rmsnorm(x, norm_mix, BF16)
    z = _matmul_nt(h, w_in_t, F32, 1024, 1024, (0, SSD_D_INNER), name="proj_z")
    xbc = _matmul_nt(h, w_in_t, F32, 1024, 1024, (SSD_D_INNER, SSD_CONV_DIM), name="proj_xbc")
    dt_raw = _matmul_nt(h, w_in_t, F32, 1024, DT_PAD, (o_dt, DT_PAD), name="proj_dt")
    qkv = _matmul_nt(h, w_qkvg_t, BF16, 1024, 1024, (0, 3 * d_sb), name="proj_qkv")

    y_ssd, gate = _ssd(z, xbc, dt_raw, conv_w, conv_b, dt_bias, a_log, d_skip, ssd_norm, h, w_qkvg_t, batch, seq,
                       gate_cols=(3 * d_sb, 2 * d))
    o_sb = _sb_attention(qkv, batch, seq, n_sb_heads)

    merged = _merge(y_ssd, o_sb, w_ssd_out.astype(BF16), w_sb_out.astype(BF16), gate,
                    b_gate.astype(F32).reshape(1, 2 * d))
    x, x_b, x_ssq = _matmul_res_stats(merged, w_out.astype(BF16), x, 1024, 512, name="out_proj")

    hm = _rmsnorm(mem, norm_mem, BF16)
    q = _matmul(x_b, w_xa_q.astype(BF16), BF16, 1024, 1024, name="xa_q", rmsnorm=(x_ssq, norm_xa))
    kv = _matmul(hm, w_xa_kv.astype(BF16), BF16, mem.shape[0], 1024, name="xa_kv")
    o = _xattn(q, kv, batch, seq, n_mem, XA_N_HEADS)
    x, x_b, x_ssq = _matmul_res_stats(o, w_xa_o.astype(BF16), x, 1024, 512, name="xa_o")

    act = _swiglu(x_b, w_ffn_in.astype(BF16), x_ssq, norm_ffn)
    x = _matmul_ksplit_res(act, w_ffn_out.astype(BF16), x, 512, 512, d_ff, name="ffn_out")
    return x


def kernel(x, mem, norm_mix, w_in, b_gate, conv_w, conv_b, dt_bias, a_log, d_skip, ssd_norm, w_ssd_out, w_sb_out, w_out, norm_xa, norm_mem, w_xa_q, w_xa_kv, w_xa_o, norm_ffn, w_ffn_in, w_ffn_out, norm_final):
    batch, seq, d = x.shape
    xf = x.reshape(batch * seq, d)
    memf = mem.reshape(-1, d)
    for l in range(norm_mix.shape[0]):
        xf = _layer(xf, memf, norm_mix[l], w_in[l], b_gate[l], conv_w[l], conv_b[l], dt_bias[l], a_log[l],
                    d_skip[l], ssd_norm[l], w_ssd_out[l], w_sb_out[l], w_out[l], norm_xa[l], norm_mem[l],
                    w_xa_q[l], w_xa_kv[l], w_xa_o[l], norm_ffn[l], w_ffn_in[l], w_ffn_out[l], batch, seq)
    out = _rmsnorm(xf, norm_final, x.dtype)
    return out.reshape(batch, seq, d)
```
